```python
import math
import jax
import jax.numpy as jnp
from jax import lax
import numpy as np

D_MODEL = 1024
BATCH = 2
SEQ = 8192
DEPTH = 2

GRID_W = 64
CTX_LEN = 256
EPS = 1e-6
F32 = jnp.float32

F_GROUPS = 4
F_GROUP_DIM = D_MODEL // 16
F_WIDTH = F_GROUPS * F_GROUP_DIM
DA_HEADS = 4
DA_QK_DIM = D_MODEL // 16
DA_V_DIM = 2 * DA_QK_DIM
DA_QK_WIDTH = DA_HEADS * 2 * DA_QK_DIM
DA_WIDTH = DA_HEADS * DA_V_DIM
ATT_SCALE = DA_QK_DIM ** -0.5
QBLOCK = 128
ROPE_BASE = 10000.0
CV_WIDTH = D_MODEL // 4
CONV_TAPS = 31
POOL_WINDOWS = (2, 4, 8, 16)
P_GROUP_DIM = D_MODEL // 16
P_WIDTH = len(POOL_WINDOWS) * P_GROUP_DIM
N_BRANCH = 4
D_FF = ((8 * D_MODEL // 3 + 127) // 128) * 128
FFN_TAPS = 3

OFF_F = 0
OFF_Q = OFF_F + F_WIDTH
OFF_K = OFF_Q + DA_QK_WIDTH
OFF_V = OFF_K + DA_QK_WIDTH
OFF_C = OFF_V + DA_WIDTH
OFF_P = OFF_C + 2 * CV_WIDTH
OFF_G = OFF_P + P_WIDTH
IN_WIDTH = OFF_G + N_BRANCH * D_MODEL

kernel_name = "hybrid_fnet_diffattn_conformer_pool_dit"


def rmsnorm(x, g):
    xf = x.astype(F32)
    y = xf * lax.rsqrt(jnp.mean(xf * xf, axis=-1, keepdims=True) + EPS)
    return (y * g.astype(F32)).astype(x.dtype)


def layernorm(x, g, b):
    xf = x.astype(F32)
    mu = jnp.mean(xf, axis=-1, keepdims=True)
    var = jnp.mean(jnp.square(xf - mu), axis=-1, keepdims=True)
    return ((xf - mu) * lax.rsqrt(var + EPS) * g.astype(F32) + b.astype(F32)).astype(x.dtype)


def dwconv(x, w, b):
    taps = w.shape[0]
    pad = (taps - 1) // 2
    y = lax.conv_general_dilated(x, w[:, None, :].astype(x.dtype), (1,), [(pad, taps - 1 - pad)],
                                 dimension_numbers=('NWC', 'WIO', 'NWC'),
                                 feature_group_count=x.shape[-1])
    return y + b


def axial_tables(rows, dtype):
    nf = DA_QK_DIM // 4
    inv = ROPE_BASE ** (-jnp.arange(nf, dtype=F32) / nf)
    r = jnp.repeat(jnp.arange(rows, dtype=F32), GRID_W)
    col = jnp.tile(jnp.arange(GRID_W, dtype=F32), rows)
    ar = r[:, None] * inv
    ac = col[:, None] * inv
    sh = (rows * GRID_W, 1, 1, nf)
    return (jnp.cos(ar).reshape(sh).astype(dtype), jnp.sin(ar).reshape(sh).astype(dtype),
            jnp.cos(ac).reshape(sh).astype(dtype), jnp.sin(ac).reshape(sh).astype(dtype))


def rot_half(x, cos, sin):
    m = x.shape[-1] // 2
    x1, x2 = x[..., :m], x[..., m:]
    return jnp.concatenate([x1 * cos - x2 * sin, x2 * cos + x1 * sin], axis=-1)


def axial_rope(x, rope):
    cos_r, sin_r, cos_c, sin_c = rope
    n = x.shape[-1] // 2
    return jnp.concatenate([rot_half(x[..., :n], cos_r, sin_r), rot_half(x[..., n:], cos_c, sin_c)], axis=-1)


def diff_attention(q, k, v, lam):
    b, lq = q.shape[0], q.shape[1]
    nb = lq // QBLOCK
    qb = (q * ATT_SCALE).reshape(b, nb, QBLOCK, DA_HEADS, 2, DA_QK_DIM).swapaxes(0, 1)

    def block(qi):
        s = jnp.einsum('bqhmd,bkhmd->bhmqk', qi, k).astype(F32)
        pr = jax.nn.softmax(s, axis=-1)
        w = pr[:, :, 0] - lam * pr[:, :, 1]
        return jnp.einsum('bhqk,bkhd->bqhd', w.astype(v.dtype), v)

    o = lax.map(block, qb)
    return o.swapaxes(0, 1).reshape(b, lq, DA_HEADS, DA_V_DIM)


def fourier_mix(u):
    b, l, _ = u.shape
    ug = u.astype(F32).reshape(b, l, F_GROUPS, F_GROUP_DIM)
    y = jnp.real(jnp.fft.fft2(ug, axes=(1, 3), norm='ortho'))
    return y.reshape(b, l, F_WIDTH).astype(u.dtype)


def conformer_conv(u, dw_w, dw_b, ln_g, ln_b):
    a, gt = jnp.split(u, 2, axis=-1)
    z = a * jax.nn.sigmoid(gt)
    z = dwconv(z, dw_w, dw_b)
    z = layernorm(z, ln_g, ln_b)
    return jax.nn.silu(z)


def multiscale_pool(u, pool_w, pool_scale):
    b, l, _ = u.shape
    uf = u.astype(F32)
    csum = jnp.concatenate([jnp.zeros((b, 1, P_WIDTH), F32), jnp.cumsum(uf, axis=1)], axis=1)
    t = jnp.arange(l)
    means = []
    for g, win in enumerate(POOL_WINDOWS):
        lo = jnp.clip(t - win // 2, 0, l - 1)
        hi = jnp.clip(t + win - win // 2 - 1, 0, l - 1)
        cs = csum[..., g * P_GROUP_DIM:(g + 1) * P_GROUP_DIM]
        cnt = (hi - lo + 1).astype(F32)[None, :, None]
        means.append((cs[:, hi + 1] - cs[:, lo]) / cnt)
    d = (jnp.concatenate(means, axis=-1) - uf).astype(u.dtype)
    d = d.reshape(b, l, len(POOL_WINDOWS), P_GROUP_DIM)
    y = jnp.einsum('blgc,gcd->blgd', d, pool_w).reshape(b, l, P_WIDTH)
    return y * pool_scale


def token_mixer(p, k_ctx, v_ctx, rope, lam, lam_init, subln_g, conv_dw_w, conv_dw_b, conv_ln_g, conv_ln_b,
                pool_w, pool_scale, wo_f, wo_a, wo_c, wo_p, w_out):
    b, l, _ = p.shape
    u_f = p[..., OFF_F:OFF_Q]
    q = p[..., OFF_Q:OFF_K].reshape(b, l, DA_HEADS, 2, DA_QK_DIM)
    k = p[..., OFF_K:OFF_V].reshape(b, l, DA_HEADS, 2, DA_QK_DIM)
    v = p[..., OFF_V:OFF_C].reshape(b, l, DA_HEADS, DA_V_DIM)
    u_c = p[..., OFF_C:OFF_P]
    u_p = p[..., OFF_P:OFF_G]
    gates = jax.nn.sigmoid(p[..., OFF_G:].astype(F32)).astype(p.dtype).reshape(b, l, N_BRANCH, D_MODEL)
    if rope is None:
        keys, vals = k, v
    else:
        q = axial_rope(q, rope)
        keys = jnp.concatenate([k_ctx, axial_rope(k, rope)], axis=1)
        vals = jnp.concatenate([v_ctx, v], axis=1)
    o_a = diff_attention(q, keys, vals, lam)
    o_a = (rmsnorm(o_a, subln_g) * (1.0 - lam_init)).reshape(b, l, DA_WIDTH)
    y = (gates[:, :, 0] * (fourier_mix(u_f) @ wo_f)
         + gates[:, :, 1] * (o_a @ wo_a)
         + gates[:, :, 2] * (conformer_conv(u_c, conv_dw_w, conv_dw_b, conv_ln_g, conv_ln_b) @ wo_c)
         + gates[:, :, 3] * (multiscale_pool(u_p, pool_w, pool_scale) @ wo_p))
    return y @ w_out


def conv_ffn(h, w_up, dw_w, dw_b, w_down):
    val, gt = jnp.split(h @ w_up, 2, axis=-1)
    gt = jax.nn.gelu(dwconv(gt, dw_w, dw_b), approximate=False)
    return (val * gt) @ w_down


def setup_inputs(seed: int = 0) -> dict:
    key = jax.random.key(seed)
    ks = jax.random.split(key, 30)

    def nrm(i, shape, s):
        return jax.random.normal(ks[i], shape, F32) * s

    L, D = DEPTH, D_MODEL
    return {
        "x": nrm(0, (BATCH, SEQ, D), 1.0),
        "c": nrm(1, (BATCH, D), 1.0),
        "ctx": nrm(2, (BATCH, CTX_LEN, D), 1.0),
        "c_ctx": nrm(3, (D,), 1.0),
        "norm1_g": 1.0 + nrm(4, (L, D), 0.02),
        "norm2_g": 1.0 + nrm(5, (L, D), 0.02),
        "ada_w": nrm(6, (L, D, 6 * D), 0.5 * D ** -0.5),
        "ada_b": nrm(7, (L, 6 * D), 0.02),
        "w_in": nrm(8, (L, D, IN_WIDTH), D ** -0.5),
        "lam_q1": nrm(9, (L, DA_QK_DIM), 0.1),
        "lam_k1": nrm(10, (L, DA_QK_DIM), 0.1),
        "lam_q2": nrm(11, (L, DA_QK_DIM), 0.1),
        "lam_k2": nrm(12, (L, DA_QK_DIM), 0.1),
        "subln_g": 1.0 + nrm(13, (L, DA_V_DIM), 0.02),
        "conv_dw_w": nrm(14, (L, CONV_TAPS, CV_WIDTH), CONV_TAPS ** -0.5),
        "conv_dw_b": nrm(15, (L, CV_WIDTH), 0.02),
        "conv_ln_g": 1.0 + nrm(16, (L, CV_WIDTH), 0.02),
        "conv_ln_b": nrm(17, (L, CV_WIDTH), 0.02),
        "pool_w": nrm(18, (L, len(POOL_WINDOWS), P_GROUP_DIM, P_GROUP_DIM), P_GROUP_DIM ** -0.5),
        "pool_scale": 1.0 + nrm(19, (L, P_WIDTH), 0.02),
        "wo_f": nrm(20, (L, F_WIDTH, D), F_WIDTH ** -0.5),
        "wo_a": nrm(21, (L, DA_WIDTH, D), DA_WIDTH ** -0.5),
        "wo_c": nrm(22, (L, CV_WIDTH, D), CV_WIDTH ** -0.5),
        "wo_p": nrm(23, (L, P_WIDTH, D), P_WIDTH ** -0.5),
        "w_out": nrm(24, (L, D, D), D ** -0.5),
        "w_up": nrm(25, (L, D, 2 * D_FF), D ** -0.5),
        "ffn_dw_w": nrm(26, (L, FFN_TAPS, D_FF), FFN_TAPS ** -0.5),
        "ffn_dw_b": nrm(27, (L, D_FF), 0.02),
        "w_down": nrm(28, (L, D_FF, D), D_FF ** -0.5),
        "final_g": 1.0 + nrm(29, (D,), 0.02),
    }


def reference(x, c, ctx, c_ctx, norm1_g, norm2_g, ada_w, ada_b, w_in, lam_q1, lam_k1, lam_q2, lam_k2,
              subln_g, conv_dw_w, conv_dw_b, conv_ln_g, conv_ln_b, pool_w, pool_scale, wo_f, wo_a, wo_c,
              wo_p, w_out, w_up, ffn_dw_w, ffn_dw_b, w_down, final_g):
    b = x.shape[0]
    n_ctx = ctx.shape[1]
    rows = x.shape[1] // GRID_W
    rope = axial_tables(rows, x.dtype)
    for l in range(DEPTH):
        last = l == DEPTH - 1
        lam_init = 0.8 - 0.6 * math.exp(-0.3 * l)
        lam = (jnp.exp(jnp.sum(lam_q1[l].astype(F32) * lam_k1[l].astype(F32)))
               - jnp.exp(jnp.sum(lam_q2[l].astype(F32) * lam_k2[l].astype(F32))) + lam_init)
        mod_x = (jax.nn.silu(c) @ ada_w[l] + ada_b[l])[:, None, :]
        mod_c = (jax.nn.silu(c_ctx) @ ada_w[l] + ada_b[l])[None, None, :]
        sh1, sc1, g1, sh2, sc2, g2 = jnp.split(mod_x, 6, axis=-1)
        csh1, csc1, cg1, csh2, csc2, cg2 = jnp.split(mod_c, 6, axis=-1)

        hc = rmsnorm(ctx, norm1_g[l]) * (1.0 + csc1) + csh1
        if last:
            kv_c = hc @ w_in[l][:, OFF_K:OFF_C]
        else:
            pc = hc @ w_in[l]
            kv_c = pc[..., OFF_K:OFF_C]
        k_c = kv_c[..., :DA_QK_WIDTH].reshape(b, n_ctx, DA_HEADS, 2, DA_QK_DIM)
        v_c = kv_c[..., DA_QK_WIDTH:].reshape(b, n_ctx, DA_HEADS, DA_V_DIM)

        hx = rmsnorm(x, norm1_g[l]) * (1.0 + sc1) + sh1
        x = x + g1 * token_mixer(hx @ w_in[l], k_c, v_c, rope, lam, lam_init, subln_g[l], conv_dw_w[l],
                                 conv_dw_b[l], conv_ln_g[l], conv_ln_b[l], pool_w[l], pool_scale[l],
                                 wo_f[l], wo_a[l], wo_c[l], wo_p[l], w_out[l])
        hx2 = rmsnorm(x, norm2_g[l]) * (1.0 + sc2) + sh2
        x = x + g2 * conv_ffn(hx2, w_up[l], ffn_dw_w[l], ffn_dw_b[l], w_down[l])

        if not last:
            ctx = ctx + cg1 * token_mixer(pc, None, None, None, lam, lam_init, subln_g[l], conv_dw_w[l],
                                          conv_dw_b[l], conv_ln_g[l], conv_ln_b[l], pool_w[l], pool_scale[l],
                                          wo_f[l], wo_a[l], wo_c[l], wo_p[l], w_out[l])
            hc2 = rmsnorm(ctx, norm2_g[l]) * (1.0 + csc2) + csh2
            ctx = ctx + cg2 * conv_ffn(hc2, w_up[l], ffn_dw_w[l], ffn_dw_b[l], w_down[l])
    return rmsnorm(x, final_g)
```

```python
import functools
import math

import numpy as np
import jax
import jax.numpy as jnp
from jax import lax
from jax.experimental import pallas as pl
from jax.experimental.pallas import tpu as pltpu

F32 = jnp.float32
BF16 = jnp.bfloat16

EPS = 1e-6
GRID_W = 64
ROPE_BASE = 10000.0
N_HEADS = 4
QK_DIM = 64
HEAD_W = 2 * QK_DIM
ATT_SCALE = QK_DIM ** -0.5
ROPE_HALF = QK_DIM // 2
CONV_TAPS = 31
POOL_WINDOWS = (2, 4, 8, 16)
POOL_GROUP = 64
F_GROUP = 64
FFN_TAPS = 3
HALO = 16
DFT_N1 = 128
FFN_CHUNK = 256

VMEM_LIMIT_BYTES = 56 * 1024 * 1024


def _cparams(n_axes):
    return pltpu.CompilerParams(dimension_semantics=("arbitrary",) * n_axes,
                                vmem_limit_bytes=VMEM_LIMIT_BYTES)


def _resident(shape):
    nd = len(shape)
    return pl.BlockSpec(shape, lambda *_: (0,) * nd, pipeline_mode=pl.Buffered(1))


def _dot(a, b):
    return jnp.dot(a, b, preferred_element_type=F32)


def _dot_nt(a, b):
    return lax.dot_general(a, b, (((1,), (1,)), ((), ())), preferred_element_type=F32)


def _ada_kernel(cv_ref, w_ref, b_ref, o_ref):
    cv = cv_ref[...]
    s = cv * jax.nn.sigmoid(cv)
    o_ref[...] = jnp.dot(s, w_ref[...], preferred_element_type=F32,
                         precision=lax.Precision.HIGHEST) + b_ref[...]


def _ada_mod(cv, ada_w, ada_b):
    depth, d, d6 = ada_w.shape
    tn = 1536
    return pl.pallas_call(
        _ada_kernel,
        grid=(depth, d6 // tn),
        in_specs=[pl.BlockSpec((8, d), lambda l, j: (0, 0)),
                  pl.BlockSpec((None, d, tn), lambda l, j: (l, 0, j)),
                  pl.BlockSpec((None, 1, tn), lambda l, j: (l, 0, j))],
        out_specs=pl.BlockSpec((None, 8, tn), lambda l, j: (l, 0, j)),
        out_shape=jax.ShapeDtypeStruct((depth, 8, d6), F32),
        compiler_params=_cparams(2),
        name="ada_mod",
    )(cv, ada_w, ada_b.reshape(depth, 1, d6))


def _modnorm(x, gain, shift, scale):
    ms = jnp.mean(x * x, axis=-1, keepdims=True)
    return x * lax.rsqrt(ms + EPS) * gain * (1.0 + scale) + shift


def _in_proj_kernel(*refs, rope, n_gate_chunks):
    if rope:
        (x_ref, mod_ref, g_ref, wf_ref, wk_ref, wqt_ref, wvt_ref, wcp_ref, wg_ref,
         kc_ref, ksa_ref, ksb_ref, qc_ref, qs_ref,
         uf_ref, k_ref, qt_ref, vt_ref, ucp_ref, gate_ref) = refs
    else:
        (x_ref, mod_ref, g_ref, wf_ref, wk_ref, wqt_ref, wvt_ref, wcp_ref, wg_ref,
         uf_ref, k_ref, qt_ref, vt_ref, ucp_ref, gate_ref) = refs
    h = _modnorm(x_ref[...], g_ref[...], mod_ref[0:1, :], mod_ref[1:2, :]).astype(BF16)
    tm = h.shape[0]

    uf_ref[...] = _dot(h, wf_ref[...])
    ucp_ref[...] = _dot(h, wcp_ref[...]).astype(BF16)
    vt_ref[...] = _dot_nt(wvt_ref[...], h).astype(BF16)

    k = _dot(h, wk_ref[...])
    qt = _dot_nt(wqt_ref[...], h)
    if rope:
        kc, ksa, ksb = kc_ref[...], ksa_ref[...], ksb_ref[...]
        for hd in range(N_HEADS):
            kh = k[:, hd * HEAD_W:(hd + 1) * HEAD_W]
            kh = kh * kc + pltpu.roll(kh, HEAD_W - ROPE_HALF, 1) * ksb + pltpu.roll(kh, ROPE_HALF, 1) * ksa
            k_ref[:, hd * HEAD_W:(hd + 1) * HEAD_W] = kh.astype(BF16)
        q3 = qt.reshape(2 * N_HEADS, QK_DIM, tm)
        q3s = jnp.concatenate([q3[:, ROPE_HALF:, :], q3[:, :ROPE_HALF, :]], axis=1)
        q3 = q3 * qc_ref[...][None] + q3s * qs_ref[...][None]
        qt = q3.reshape(2 * N_HEADS * QK_DIM, tm)
    else:
        k_ref[...] = k.astype(BF16)
    qt_ref[...] = (qt * ATT_SCALE).astype(BF16)

    gw = wg_ref.shape[1] // n_gate_chunks
    for c in range(n_gate_chunks):
        g = _dot(h, wg_ref[:, c * gw:(c + 1) * gw])
        gate_ref[:, c * gw:(c + 1) * gw] = jax.nn.sigmoid(g).astype(BF16)


def _in_proj(x, mod, gain, w, rope_tabs, *, tm, seq_len):
    n, d = x.shape
    tps = seq_len // tm
    nb = mod.shape[0]
    rope = rope_tabs is not None
    mod_map = (lambda i: (i // tps, 0, 0)) if nb > 1 else (lambda i: (0, 0, 0))
    in_specs = [pl.BlockSpec((tm, d), lambda i: (i, 0)),
                pl.BlockSpec((None, 6, d), mod_map),
                _resident((1, d))] + [_resident(a.shape) for a in w]
    args = [x, mod, gain.reshape(1, d)] + list(w)
    if rope:
        kc, ksa, ksb, qc, qs = rope_tabs
        in_specs += [pl.BlockSpec((tm, HEAD_W), lambda i: (i % tps, 0))] * 3
        in_specs += [pl.BlockSpec((QK_DIM, tm), lambda i: (0, i % tps))] * 2
        args += [kc, ksa, ksb, qc, qs]
    wf, wk, wqt, wvt, wcp, wg = w
    out_shape = (jax.ShapeDtypeStruct((n, wf.shape[1]), F32),
                 jax.ShapeDtypeStruct((n, wk.shape[1]), BF16),
                 jax.ShapeDtypeStruct((wqt.shape[0], n), BF16),
                 jax.ShapeDtypeStruct((wvt.shape[0], n), BF16),
                 jax.ShapeDtypeStruct((n, wcp.shape[1]), BF16),
                 jax.ShapeDtypeStruct((n, wg.shape[1]), BF16))
    out_specs = (pl.BlockSpec((tm, wf.shape[1]), lambda i: (i, 0)),
                 pl.BlockSpec((tm, wk.shape[1]), lambda i: (i, 0)),
                 pl.BlockSpec((wqt.shape[0], tm), lambda i: (0, i)),
                 pl.BlockSpec((wvt.shape[0], tm), lambda i: (0, i)),
                 pl.BlockSpec((tm, wcp.shape[1]), lambda i: (i, 0)),
                 pl.BlockSpec((tm, wg.shape[1]), lambda i: (i, 0)))
    return pl.pallas_call(
        functools.partial(_in_proj_kernel, rope=rope, n_gate_chunks=4),
        grid=(n // tm,), in_specs=in_specs, out_specs=out_specs, out_shape=out_shape,
        compiler_params=_cparams(1), name="in_proj_rope" if rope else "in_proj",
    )(*args)


def _fnet_kernel(x_ref, t1_ref, g_ref, cs_ref, o_ref, zz_ref, *, n2, scale):
    def stage1(n1, carry):
        xs = x_ref[pl.ds(n1, n2, stride=DFT_N1), :].astype(BF16)
        z = _dot(t1_ref[n1], xs)
        zz_ref[pl.ds(n1, n2, stride=2 * DFT_N1), :] = z[:n2]
        zz_ref[pl.ds(DFT_N1 + n1, n2, stride=2 * DFT_N1), :] = z[n2:]
        return carry

    lax.fori_loop(0, DFT_N1, stage1, 0)

    def stage2(k2, carry):
        base = pl.multiple_of(k2 * (2 * DFT_N1), 2 * DFT_N1)
        zk = zz_ref[pl.ds(base, 2 * DFT_N1), :].astype(BF16)
        p = _dot(g_ref[...], zk)
        pc = jnp.concatenate([p[:DFT_N1], p[DFT_N1:]], axis=1).astype(BF16)
        y = _dot(pc, cs_ref[...]) * scale
        o_ref[pl.ds(k2, DFT_N1, stride=n2), :] = y
        return carry

    lax.fori_loop(0, n2, stage2, 0)


def _channel_dft_mats(ch):
    c = np.arange(F_GROUP)
    ang = 2.0 * np.pi * np.outer(c, c) / F_GROUP
    eye = np.eye(ch // F_GROUP)
    return np.kron(eye, np.cos(ang)), np.kron(eye, np.sin(ang))


def _fnet(u, *, seq_len):
    n, width = u.shape
    nb = n // seq_len
    ch = 128
    n2 = seq_len // DFT_N1
    n1i, k2i, n2i = np.arange(DFT_N1), np.arange(n2), np.arange(n2)
    theta = 2.0 * np.pi * (n1i[:, None, None] * k2i[None, :, None] / seq_len
                           + k2i[None, :, None] * n2i[None, None, :] / n2)
    t1 = np.concatenate([np.cos(theta), -np.sin(theta)], axis=1)
    k1i = np.arange(DFT_N1)
    phi = 2.0 * np.pi * np.outer(k1i, n1i) / DFT_N1
    cg, sg = np.cos(phi), np.sin(phi)
    gmat = np.block([[cg, sg], [-sg, cg]])
    cbd, sbd = _channel_dft_mats(ch)
    cs = np.concatenate([cbd, sbd], axis=0)
    scale = 1.0 / math.sqrt(seq_len * F_GROUP)
    return pl.pallas_call(
        functools.partial(_fnet_kernel, n2=n2, scale=scale),
        grid=(nb, width // ch),
        in_specs=[pl.BlockSpec((seq_len, ch), lambda b, c: (b, c)),
                  _resident(t1.shape), _resident(gmat.shape), _resident(cs.shape)],
        out_specs=pl.BlockSpec((seq_len, ch), lambda b, c: (b, c)),
        out_shape=jax.ShapeDtypeStruct((n, width), F32),
        scratch_shapes=[pltpu.VMEM((n2 * 2 * DFT_N1, ch), F32)],
        compiler_params=_cparams(2), name="fnet",
    )(u, jnp.asarray(t1, BF16), jnp.asarray(gmat, BF16), jnp.asarray(cs, BF16))


def _fnet_dense_kernel(x_ref, f_ref, cs_ref, o_ref, *, scale):
    p = _dot(f_ref[...], x_ref[...].astype(BF16))
    l = p.shape[0] // 2
    pc = jnp.concatenate([p[:l], p[l:]], axis=1).astype(BF16)
    o_ref[...] = _dot(pc, cs_ref[...]) * scale


def _fnet_dense(u, *, seq_len):
    n, width = u.shape
    pos = np.arange(seq_len)
    ang = 2.0 * np.pi * np.outer(pos, pos) / seq_len
    fmat = np.concatenate([np.cos(ang), np.sin(ang)], axis=0)
    cbd, sbd = _channel_dft_mats(width)
    cs = np.concatenate([cbd, -sbd], axis=0)
    scale = 1.0 / math.sqrt(seq_len * F_GROUP)
    return pl.pallas_call(
        functools.partial(_fnet_dense_kernel, scale=scale),
        grid=(n // seq_len,),
        in_specs=[pl.BlockSpec((seq_len, width), lambda b: (b, 0)),
                  _resident(fmat.shape), _resident(cs.shape)],
        out_specs=pl.BlockSpec((seq_len, width), lambda b: (b, 0)),
        out_shape=jax.ShapeDtypeStruct((n, width), F32),
        compiler_params=_cparams(1), name="fnet_dense",
    )(u, jnp.asarray(fmat, BF16), jnp.asarray(cs, BF16))


def _attn_kernel(qt_ref, k_ref, vt_ref, lq1_ref, lk1_ref, lq2_ref, lk2_ref, sg_ref, o_ref, *,
                 tk, lam_init):
    tq = qt_ref.shape[1]
    lk = k_ref.shape[0]
    lam = (jnp.exp(jnp.sum(lq1_ref[...] * lk1_ref[...], axis=1, keepdims=True))
           - jnp.exp(jnp.sum(lq2_ref[...] * lk2_ref[...], axis=1, keepdims=True)) + lam_init)

    qt = qt_ref[...]
    row = lax.broadcasted_iota(jnp.int32, qt.shape, 0)
    zero = jnp.zeros_like(qt)
    qm = jnp.concatenate([jnp.where(row < QK_DIM, qt, zero),
                          jnp.where(row >= QK_DIM, qt, zero)], axis=1)

    m = jnp.full((1, 2 * tq), -jnp.inf, F32)
    l = jnp.zeros((1, 2 * tq), F32)
    acc = jnp.zeros((HEAD_W, 2 * tq), F32)
    for c in range(lk // tk):
        s = _dot(k_ref[c * tk:(c + 1) * tk, :], qm)
        m_new = jnp.maximum(m, jnp.max(s, axis=0, keepdims=True))
        alpha = jnp.exp(m - m_new)
        e = jnp.exp(s - m_new)
        l = alpha * l + jnp.sum(e, axis=0, keepdims=True)
        acc = acc * alpha + _dot(vt_ref[:, c * tk:(c + 1) * tk], e.astype(BF16))
        m = m_new

    inv = 1.0 / l
    o = acc[:, :tq] * inv[:, :tq] - lam * (acc[:, tq:] * inv[:, tq:])
    ms = jnp.mean(o * o, axis=0, keepdims=True)
    y = o * lax.rsqrt(ms + EPS) * sg_ref[...] * (1.0 - lam_init)
    o_ref[...] = y.T.astype(BF16)


def _attention(qt, k, vt, lam_vecs, subln_g, *, lq, lk, tq, tk, lam_init):
    n = qt.shape[1]
    nb = n // lq
    nq = lq // tq
    vec = pl.BlockSpec((1, QK_DIM), lambda b, h, i: (0, 0))
    return pl.pallas_call(
        functools.partial(_attn_kernel, tk=tk, lam_init=lam_init),
        grid=(nb, N_HEADS, nq),
        in_specs=[pl.BlockSpec((HEAD_W, tq), lambda b, h, i: (h, b * nq + i)),
                  pl.BlockSpec((lk, HEAD_W), lambda b, h, i: (b, h)),
                  pl.BlockSpec((HEAD_W, lk), lambda b, h, i: (h, b)),
                  vec, vec, vec, vec,
                  pl.BlockSpec((HEAD_W, 1), lambda b, h, i: (0, 0))],
        out_specs=pl.BlockSpec((tq, HEAD_W), lambda b, h, i: (b * nq + i, h)),
        out_shape=jax.ShapeDtypeStruct((n, N_HEADS * HEAD_W), BF16),
        compiler_params=_cparams(3), name="diff_attn",
    )(qt, k, vt, *[v.reshape(1, QK_DIM) for v in lam_vecs], subln_g.reshape(HEAD_W, 1))


def _seq_flags(i, tps):
    t = i % tps
    return t == 0, t == tps - 1


def _local_kernel(u_ref, up_ref, un_ref, dww_ref, dwb_ref, lng_ref, lnb_ref, pw_ref, ps_ref,
                  cc_ref, pp_ref, zc_ref, zp_ref, *, tps, seq_len):
    tm = u_ref.shape[0]
    cw = cc_ref.shape[1]
    first, last = _seq_flags(pl.program_id(0), tps)

    def split(u):
        u = u.astype(F32)
        return u[:, :cw] * jax.nn.sigmoid(u[:, cw:2 * cw]), u[:, 2 * cw:]

    zc, zp = split(u_ref[...])
    zc_ref[HALO:HALO + tm, :] = zc
    zp_ref[HALO:HALO + tm, :] = zp
    zc, zp = split(up_ref[...])
    keep = jnp.where(first, 0.0, 1.0)
    zc_ref[0:HALO, :] = zc * keep
    zp_ref[0:HALO, :] = zp * keep
    zc, zp = split(un_ref[...])
    keep = jnp.where(last, 0.0, 1.0)
    zc_ref[HALO + tm:, :] = zc * keep
    zp_ref[HALO + tm:, :] = zp * keep

    pad = (CONV_TAPS - 1) // 2
    acc = jnp.zeros((tm, cw), F32) + dwb_ref[...]
    for j in range(CONV_TAPS):
        acc = acc + zc_ref[pl.ds(HALO - pad + j, tm), :] * dww_ref[j:j + 1, :]
    mu = jnp.mean(acc, axis=-1, keepdims=True)
    var = jnp.mean(jnp.square(acc - mu), axis=-1, keepdims=True)
    y = (acc - mu) * lax.rsqrt(var + EPS) * lng_ref[...] + lnb_ref[...]
    cc_ref[...] = (y * jax.nn.sigmoid(y)).astype(BF16)

    def shifted(d):
        return zp_ref[pl.ds(HALO + d, tm), :]

    centre = shifted(0)
    sums = []
    s = None
    prev_half = 0
    for win in POOL_WINDOWS:
        half = win // 2
        for d in list(range(-half, -prev_half)) + list(range(prev_half, half)):
            term = centre if d == 0 else shifted(d)
            s = term if s is None else s + term
        sums.append(s)
        prev_half = half
    pos = (pl.program_id(0) % tps) * tm + lax.broadcasted_iota(jnp.int32, (tm, 1), 0)
    lane_group = lax.broadcasted_iota(jnp.int32, (tm, cw), 1) // POOL_GROUP
    mean = None
    for g, win in enumerate(POOL_WINDOWS):
        half = win // 2
        hi = jnp.minimum(pos + (win - half - 1), seq_len - 1)
        lo = jnp.maximum(pos - half, 0)
        mg = sums[g] / (hi - lo + 1).astype(F32)
        mean = mg if mean is None else jnp.where(lane_group == g, mg, mean)
    d = (mean - centre).astype(BF16)
    pp_ref[...] = (_dot(d, pw_ref[...]) * ps_ref[...]).astype(BF16)


def _local_mix(ucp, dw_w, dw_b, ln_g, ln_b, pool_bd, pool_scale, *, tm, seq_len):
    n, w3 = ucp.shape
    cw = w3 // 3
    tps = seq_len // tm
    r = tm // HALO
    nblk = n // HALO
    row = lambda a: a.reshape(1, cw)
    return pl.pallas_call(
        functools.partial(_local_kernel, tps=tps, seq_len=seq_len),
        grid=(n // tm,),
        in_specs=[pl.BlockSpec((tm, w3), lambda i: (i, 0)),
                  pl.BlockSpec((HALO, w3), lambda i: (jnp.maximum(i * r - 1, 0), 0)),
                  pl.BlockSpec((HALO, w3), lambda i: (jnp.minimum((i + 1) * r, nblk - 1), 0)),
                  _resident((CONV_TAPS, cw)), _resident((1, cw)), _resident((1, cw)), _resident((1, cw)),
                  _resident((cw, cw)), _resident((1, cw))],
        out_specs=(pl.BlockSpec((tm, cw), lambda i: (i, 0)), pl.BlockSpec((tm, cw), lambda i: (i, 0))),
        out_shape=(jax.ShapeDtypeStruct((n, cw), BF16), jax.ShapeDtypeStruct((n, cw), BF16)),
        scratch_shapes=[pltpu.VMEM((tm + 2 * HALO, cw), F32), pltpu.VMEM((tm + 2 * HALO, cw), F32)],
        compiler_params=_cparams(1), name="local_mix",
    )(ucp, ucp, ucp, dw_w, row(dw_b), row(ln_g), row(ln_b), pool_bd, row(pool_scale))


def _merge_kernel(x_ref, mod_ref, fm_ref, oa_ref, cc_ref, pp_ref, gate_ref,
                  wof_ref, woa_ref, woc_ref, wop_ref, wout_ref, o_ref):
    d = x_ref.shape[1]
    branches = ((fm_ref[...].astype(BF16), wof_ref), (oa_ref[...], woa_ref),
                (cc_ref[...], woc_ref), (pp_ref[...], wop_ref))
    y = None
    for i, (act, w_ref) in enumerate(branches):
        t = gate_ref[:, i * d:(i + 1) * d].astype(F32) * _dot(act, w_ref[...])
        y = t if y is None else y + t
    z = _dot(y.astype(BF16), wout_ref[...])
    o_ref[...] = x_ref[...] + mod_ref[2:3, :] * z


def _merge(x, mod, fm, oa, cc, pp, gates, w, *, tm, seq_len):
    n, d = x.shape
    tps = seq_len // tm
    nb = mod.shape[0]
    mod_map = (lambda i: (i // tps, 0, 0)) if nb > 1 else (lambda i: (0, 0, 0))
    tile = lambda a: pl.BlockSpec((tm, a.shape[1]), lambda i: (i, 0))
    return pl.pallas_call(
        _merge_kernel,
        grid=(n // tm,),
        in_specs=[tile(x), pl.BlockSpec((None, 6, d), mod_map), tile(fm), tile(oa), tile(cc), tile(pp),
                  tile(gates)] + [_resident(a.shape) for a in w],
        out_specs=tile(x),
        out_shape=jax.ShapeDtypeStruct((n, d), F32),
        compiler_params=_cparams(1), name="merge",
    )(x, mod, fm, oa, cc, pp, gates, *w)


def _ffn_kernel(x_ref, xp_ref, xn_ref, mod_ref, g_ref, wv_ref, wg_ref, dww_ref, dwb_ref, wd_ref, fg_ref,
                o_ref, h_ref, gs_ref, mid_ref, *, tps, final):
    tm = x_ref.shape[0]
    first, last = _seq_flags(pl.program_id(0), tps)
    gain, shift, scale = g_ref[...], mod_ref[3:4, :], mod_ref[4:5, :]
    x = x_ref[...]
    h_ref[HALO:HALO + tm, :] = _modnorm(x, gain, shift, scale).astype(BF16)
    h_ref[0:HALO, :] = _modnorm(xp_ref[...], gain, shift, scale).astype(BF16)
    h_ref[HALO + tm:, :] = _modnorm(xn_ref[...], gain, shift, scale).astype(BF16)
    keep_p = jnp.where(first, 0.0, 1.0)
    keep_n = jnp.where(last, 0.0, 1.0)

    dff = wv_ref.shape[1]
    for c in range(dff // FFN_CHUNK):
        cols = slice(c * FFN_CHUNK, (c + 1) * FFN_CHUNK)
        val = _dot(h_ref[HALO:HALO + tm, :], wv_ref[:, cols])
        gt = _dot(h_ref[...], wg_ref[:, cols])
        gs_ref[HALO:HALO + tm, :] = gt[HALO:HALO + tm]
        gs_ref[0:HALO, :] = gt[:HALO] * keep_p
        gs_ref[HALO + tm:, :] = gt[HALO + tm:] * keep_n
        gc = dwb_ref[:, cols] + gs_ref[pl.ds(HALO, tm), :] * dww_ref[1:2, cols]
        gc = gc + gs_ref[pl.ds(HALO - 1, tm), :] * dww_ref[0:1, cols]
        gc = gc + gs_ref[pl.ds(HALO + 1, tm), :] * dww_ref[2:3, cols]
        act = 0.5 * gc * (1.0 + lax.erf(gc * (1.0 / math.sqrt(2.0))))
        mid_ref[:, cols] = (val * act).astype(BF16)

    y = x + mod_ref[5:6, :] * _dot(mid_ref[...], wd_ref[...])
    if final:
        ms = jnp.mean(y * y, axis=-1, keepdims=True)
        y = y * lax.rsqrt(ms + EPS) * fg_ref[...]
    o_ref[...] = y


def _ffn(x, mod, gain, wv, wg, dw_w, dw_b, wd, final_g, *, tm, seq_len, final):
    n, d = x.shape
    dff = wv.shape[1]
    tps = seq_len // tm
    nb = mod.shape[0]
    r = tm // HALO
    nblk = n // HALO
    mod_map = (lambda i: (i // tps, 0, 0)) if nb > 1 else (lambda i: (0, 0, 0))
    return pl.pallas_call(
        functools.partial(_ffn_kernel, tps=tps, final=final),
        grid=(n // tm,),
        in_specs=[pl.BlockSpec((tm, d), lambda i: (i, 0)),
                  pl.BlockSpec((HALO, d), lambda i: (jnp.maximum(i * r - 1, 0), 0)),
                  pl.BlockSpec((HALO, d), lambda i: (jnp.minimum((i + 1) * r, nblk - 1), 0)),
                  pl.BlockSpec((None, 6, d), mod_map),
                  _resident((1, d)), _resident(wv.shape), _resident(wg.shape),
                  _resident((FFN_TAPS, dff)), _resident((1, dff)), _resident(wd.shape), _resident((1, d))],
        out_specs=pl.BlockSpec((tm, d), lambda i: (i, 0)),
        out_shape=jax.ShapeDtypeStruct((n, d), F32),
        scratch_shapes=[pltpu.VMEM((tm + 2 * HALO, d), BF16),
                        pltpu.VMEM((tm + 2 * HALO, FFN_CHUNK), F32),
                        pltpu.VMEM((tm, dff), BF16)],
        compiler_params=_cparams(1), name="ffn_final" if final else "ffn",
    )(x, x, x, mod, gain.reshape(1, d), wv, wg, dw_w, dw_b.reshape(1, dff), wd, final_g.reshape(1, d))


def _rope_perm():
    q = QK_DIM // 4
    one = np.concatenate([np.arange(0, q), np.arange(2 * q, 3 * q), np.arange(q, 2 * q), np.arange(3 * q, 4 * q)])
    return np.concatenate([g * QK_DIM + one for g in range(2 * N_HEADS)])


def _rope_tables(seq_len):
    nf = QK_DIM // 4
    inv = ROPE_BASE ** (-np.arange(nf, dtype=np.float64) / nf)
    t = np.arange(seq_len)
    ang = np.concatenate([(t // GRID_W)[:, None] * inv, (t % GRID_W)[:, None] * inv], axis=1)
    cos, sin, zero = np.cos(ang), np.sin(ang), np.zeros_like(ang)
    rep = HEAD_W // QK_DIM
    kc = np.tile(np.concatenate([cos, cos], axis=1), (1, rep))
    ksa = np.tile(np.concatenate([zero, sin], axis=1), (1, rep))
    ksb = np.tile(np.concatenate([-sin, zero], axis=1), (1, rep))
    qc = np.concatenate([cos, cos], axis=1).T
    qs = np.concatenate([-sin, sin], axis=1).T
    return tuple(jnp.asarray(a, F32) for a in (kc, ksa, ksb, qc, qs))


def _pick_tk(lk):
    for tk in (768, 1024, 512, 256, 128):
        if lk % tk == 0:
            return tk
    raise ValueError(f"unsupported key length {lk}")


def kernel(x, c, ctx, c_ctx, norm1_g, norm2_g, ada_w, ada_b, w_in, lam_q1, lam_k1, lam_q2, lam_k2, subln_g, conv_dw_w, conv_dw_b, conv_ln_g, conv_ln_b, pool_w, pool_scale, wo_f, wo_a, wo_c, wo_p, w_out, w_up, ffn_dw_w, ffn_dw_b, w_down, final_g):
    bsz, seq, d = x.shape
    n_ctx = ctx.shape[1]
    depth = w_in.shape[0]
    d_ff = w_down.shape[1]
    f_w = wo_f.shape[1]
    a_w = wo_a.shape[1]
    c_w = wo_c.shape[1]
    p_w = wo_p.shape[1]
    off_q = f_w
    off_k = off_q + a_w
    off_v = off_k + a_w
    off_c = off_v + a_w
    off_g = off_c + 2 * c_w + p_w
    assert seq % (8 * DFT_N1) == 0 and n_ctx % 256 == 0
    assert a_w == N_HEADS * HEAD_W and d_ff % FFN_CHUNK == 0
    tm_x, tm_c = 512, 256
    lk = n_ctx + seq
    tk = _pick_tk(lk)

    cv = jnp.concatenate([c, c_ctx[None, :], jnp.zeros((8 - bsz - 1, d), F32)], axis=0)
    mods = _ada_mod(cv, ada_w, ada_b)
    rope_tabs = _rope_tables(seq)
    perm = _rope_perm()

    xs = x.reshape(bsz * seq, d)
    cs = ctx.reshape(bsz * n_ctx, d)
    out = None
    for l in range(depth):
        last = l == depth - 1
        lam_init = 0.8 - 0.6 * math.exp(-0.3 * l)
        mod_x = mods[l, :bsz].reshape(bsz, 6, d)
        mod_c = mods[l, bsz:bsz + 1].reshape(1, 6, d)
        wl = w_in[l]
        w_a = (wl[:, :off_q].astype(BF16),
               wl[:, off_k:off_v][:, perm].astype(BF16),
               wl[:, off_q:off_k][:, perm].T.astype(BF16),
               wl[:, off_v:off_c].T.astype(BF16),
               wl[:, off_c:off_g].astype(BF16),
               wl[:, off_g:].astype(BF16))
        lam_vecs = (lam_q1[l], lam_k1[l], lam_q2[l], lam_k2[l])
        pool_bd = jax.scipy.linalg.block_diag(*[pool_w[l, g] for g in range(len(POOL_WINDOWS))]).astype(BF16)
        w_m = (wo_f[l].astype(BF16), wo_a[l].astype(BF16), wo_c[l].astype(BF16), wo_p[l].astype(BF16),
               w_out[l].astype(BF16))
        wv_up = w_up[l][:, :d_ff].astype(BF16)
        wg_up = w_up[l][:, d_ff:].astype(BF16)
        wd = w_down[l].astype(BF16)
        local_w = (conv_dw_w[l], conv_dw_b[l], conv_ln_g[l], conv_ln_b[l], pool_bd, pool_scale[l])

        uf_c, k_c, qt_c, vt_c, ucp_c, gate_c = _in_proj(cs, mod_c, norm1_g[l], w_a, None, tm=tm_c, seq_len=n_ctx)
        uf_x, k_x, qt_x, vt_x, ucp_x, gate_x = _in_proj(xs, mod_x, norm1_g[l], w_a, rope_tabs, tm=tm_x, seq_len=seq)
        kcat = jnp.concatenate([k_c.reshape(bsz, n_ctx, a_w), k_x.reshape(bsz, seq, a_w)], axis=1)
        vtcat = jnp.concatenate([vt_c.reshape(a_w, bsz, n_ctx), vt_x.reshape(a_w, bsz, seq)], axis=2)
        oa_x = _attention(qt_x, kcat.reshape(bsz * lk, a_w), vtcat.reshape(a_w, bsz * lk), lam_vecs, subln_g[l],
                          lq=seq, lk=lk, tq=256, tk=tk, lam_init=lam_init)
        fm_x = _fnet(uf_x, seq_len=seq)
        cc_x, pp_x = _local_mix(ucp_x, *local_w, tm=tm_x, seq_len=seq)
        xs = _merge(xs, mod_x, fm_x, oa_x, cc_x, pp_x, gate_x, w_m, tm=tm_x, seq_len=seq)
        xs = _ffn(xs, mod_x, norm2_g[l], wv_up, wg_up, ffn_dw_w[l], ffn_dw_b[l], wd, final_g,
                  tm=tm_x, seq_len=seq, final=last)

        if not last:
            oa_c = _attention(qt_c, k_c, vt_c, lam_vecs, subln_g[l],
                              lq=n_ctx, lk=n_ctx, tq=n_ctx, tk=n_ctx, lam_init=lam_init)
            fm_c = _fnet_dense(uf_c, seq_len=n_ctx)
            cc_c, pp_c = _local_mix(ucp_c, *local_w, tm=tm_c, seq_len=n_ctx)
            cs = _merge(cs, mod_c, fm_c, oa_c, cc_c, pp_c, gate_c, w_m, tm=tm_c, seq_len=n_ctx)
            cs = _ffn(cs, mod_c, norm2_g[l], wv_up, wg_up, ffn_dw_w[l], ffn_dw_b[l], wd, final_g,
                      tm=tm_c, seq_len=n_ctx, final=False)
    return xs.reshape(bsz, seq, d)
```

```python
import functools
import math

import numpy as np
import jax
import jax.numpy as jnp
from jax import lax
from jax.experimental import pallas as pl
from jax.experimental.pallas import tpu as pltpu

F32 = jnp.float32
BF16 = jnp.bfloat16

EPS = 1e-6
GRID_W = 64
ROPE_BASE = 10000.0
N_HEADS = 4
QK_DIM = 64
HEAD_W = 2 * QK_DIM
ATT_SCALE = QK_DIM ** -0.5
LOG2E = math.log2(math.e)
ONES_ROWS = 16
V_ROWS = HEAD_W + ONES_ROWS
ROPE_HALF = QK_DIM // 2
CONV_TAPS = 31
POOL_WINDOWS = (2, 4, 8, 16)
POOL_GROUP = 64
F_GROUP = 64
FFN_TAPS = 3
HALO = 16
DFT_N1 = 128
FFN_CHUNK = 256
ATTN_KEY_CHUNK = 1024

VMEM_LIMIT_BYTES = 56 * 1024 * 1024


def _cparams(n_axes):
    return pltpu.CompilerParams(dimension_semantics=("arbitrary",) * n_axes,
                                vmem_limit_bytes=VMEM_LIMIT_BYTES)


def _resident(shape):
    nd = len(shape)
    return pl.BlockSpec(shape, lambda *_: (0,) * nd, pipeline_mode=pl.Buffered(1))


def _dot(a, b):
    return jnp.dot(a, b, preferred_element_type=F32)


def _dot_nt(a, b):
    return lax.dot_general(a, b, (((1,), (1,)), ((), ())), preferred_element_type=F32)


def _ada_kernel(cv_ref, w_ref, b_ref, o_ref):
    cv = cv_ref[...]
    s = cv * jax.nn.sigmoid(cv)
    o_ref[...] = jnp.dot(s, w_ref[...], preferred_element_type=F32,
                         precision=lax.Precision.HIGHEST) + b_ref[...]


def _ada_mod(cv, ada_w, ada_b):
    depth, d, d6 = ada_w.shape
    tn = 1536
    return pl.pallas_call(
        _ada_kernel,
        grid=(depth, d6 // tn),
        in_specs=[pl.BlockSpec((8, d), lambda l, j: (0, 0)),
                  pl.BlockSpec((None, d, tn), lambda l, j: (l, 0, j)),
                  pl.BlockSpec((None, 1, tn), lambda l, j: (l, 0, j))],
        out_specs=pl.BlockSpec((None, 8, tn), lambda l, j: (l, 0, j)),
        out_shape=jax.ShapeDtypeStruct((depth, 8, d6), F32),
        compiler_params=_cparams(2),
        name="ada_mod",
    )(cv, ada_w, ada_b.reshape(depth, 1, d6))


def _modnorm(x, gain, shift, scale):
    ms = jnp.mean(x * x, axis=-1, keepdims=True)
    return x * lax.rsqrt(ms + EPS) * gain * (1.0 + scale) + shift


def _in_proj_kernel(*refs, rope):
    if rope:
        (x_ref, mod_ref, g_ref, wf_ref, wk_ref, wqt_ref, wvt_ref, wcp_ref,
         kc_ref, ksa_ref, ksb_ref, qc_ref, qs_ref,
         uf_ref, k_ref, qt_ref, vt_ref, ucp_ref) = refs
    else:
        (x_ref, mod_ref, g_ref, wf_ref, wk_ref, wqt_ref, wvt_ref, wcp_ref,
         uf_ref, k_ref, qt_ref, vt_ref, ucp_ref) = refs
    h = _modnorm(x_ref[...], g_ref[...], mod_ref[0:1, :], mod_ref[1:2, :]).astype(BF16)
    tm = h.shape[0]

    uf_ref[...] = _dot(h, wf_ref[...])
    ucp_ref[...] = _dot(h, wcp_ref[...]).astype(BF16)
    vt = _dot_nt(wvt_ref[...], h).astype(BF16)
    for hd in range(N_HEADS):
        vt_ref[hd * V_ROWS:hd * V_ROWS + HEAD_W, :] = vt[hd * HEAD_W:(hd + 1) * HEAD_W]
        vt_ref[hd * V_ROWS + HEAD_W:(hd + 1) * V_ROWS, :] = jnp.ones((ONES_ROWS, tm), BF16)

    k = _dot(h, wk_ref[...])
    qt = _dot_nt(wqt_ref[...], h)
    if rope:
        kc, ksa, ksb = kc_ref[...], ksa_ref[...], ksb_ref[...]
        for hd in range(N_HEADS):
            kh = k[:, hd * HEAD_W:(hd + 1) * HEAD_W]
            kh = kh * kc + pltpu.roll(kh, HEAD_W - ROPE_HALF, 1) * ksb + pltpu.roll(kh, ROPE_HALF, 1) * ksa
            k_ref[:, hd * HEAD_W:(hd + 1) * HEAD_W] = kh.astype(BF16)
        q3 = qt.reshape(2 * N_HEADS, QK_DIM, tm)
        q3s = jnp.concatenate([q3[:, ROPE_HALF:, :], q3[:, :ROPE_HALF, :]], axis=1)
        q3 = q3 * qc_ref[...][None] + q3s * qs_ref[...][None]
        qt = q3.reshape(2 * N_HEADS * QK_DIM, tm)
    else:
        k_ref[...] = k.astype(BF16)
    qt_ref[...] = (qt * (ATT_SCALE * LOG2E)).astype(BF16)


def _in_proj(x, mod, gain, w, rope_tabs, *, tm, seq_len):
    n, d = x.shape
    tps = seq_len // tm
    nb = mod.shape[0]
    rope = rope_tabs is not None
    mod_map = (lambda i: (i // tps, 0, 0)) if nb > 1 else (lambda i: (0, 0, 0))
    in_specs = [pl.BlockSpec((tm, d), lambda i: (i, 0)),
                pl.BlockSpec((None, 6, d), mod_map),
                _resident((1, d))] + [_resident(a.shape) for a in w]
    args = [x, mod, gain.reshape(1, d)] + list(w)
    if rope:
        kc, ksa, ksb, qc, qs = rope_tabs
        in_specs += [pl.BlockSpec((tm, HEAD_W), lambda i: (i % tps, 0))] * 3
        in_specs += [pl.BlockSpec((QK_DIM, tm), lambda i: (0, i % tps))] * 2
        args += [kc, ksa, ksb, qc, qs]
    wf, wk, wqt, wvt, wcp = w
    out_shape = (jax.ShapeDtypeStruct((n, wf.shape[1]), F32),
                 jax.ShapeDtypeStruct((n, wk.shape[1]), BF16),
                 jax.ShapeDtypeStruct((wqt.shape[0], n), BF16),
                 jax.ShapeDtypeStruct((N_HEADS * V_ROWS, n), BF16),
                 jax.ShapeDtypeStruct((n, wcp.shape[1]), BF16))
    out_specs = (pl.BlockSpec((tm, wf.shape[1]), lambda i: (i, 0)),
                 pl.BlockSpec((tm, wk.shape[1]), lambda i: (i, 0)),
                 pl.BlockSpec((wqt.shape[0], tm), lambda i: (0, i)),
                 pl.BlockSpec((N_HEADS * V_ROWS, tm), lambda i: (0, i)),
                 pl.BlockSpec((tm, wcp.shape[1]), lambda i: (i, 0)))
    return pl.pallas_call(
        functools.partial(_in_proj_kernel, rope=rope),
        grid=(n // tm,), in_specs=in_specs, out_specs=out_specs, out_shape=out_shape,
        compiler_params=_cparams(1), name="in_proj_rope" if rope else "in_proj",
    )(*args)


def _fnet_kernel(x_ref, t1_ref, g32_ref, cs32_ref, o_ref, zz_ref, g_ref, cs_ref, *, n2, scale):
    g_ref[...] = g32_ref[...].astype(BF16)
    cs_ref[...] = cs32_ref[...].astype(BF16)

    def stage1(n1, carry):
        xs = x_ref[pl.ds(n1, n2, stride=DFT_N1), :].astype(BF16)
        z = _dot(t1_ref[n1].astype(BF16), xs)
        zz_ref[pl.ds(n1, n2, stride=2 * DFT_N1), :] = z[:n2]
        zz_ref[pl.ds(DFT_N1 + n1, n2, stride=2 * DFT_N1), :] = z[n2:]
        return carry

    lax.fori_loop(0, DFT_N1, stage1, 0, unroll=8)

    def stage2(k2, carry):
        base = pl.multiple_of(k2 * (2 * DFT_N1), 2 * DFT_N1)
        zk = zz_ref[pl.ds(base, 2 * DFT_N1), :].astype(BF16)
        p = _dot(g_ref[...], zk)
        pc = jnp.concatenate([p[:DFT_N1], p[DFT_N1:]], axis=1).astype(BF16)
        y = _dot(pc, cs_ref[...]) * scale
        o_ref[pl.ds(k2, DFT_N1, stride=n2), :] = y
        return carry

    lax.fori_loop(0, n2, stage2, 0, unroll=8)


def _channel_dft_mats(ch):
    c = np.arange(F_GROUP)
    ang = 2.0 * np.pi * np.outer(c, c) / F_GROUP
    eye = np.eye(ch // F_GROUP)
    return np.kron(eye, np.cos(ang)), np.kron(eye, np.sin(ang))


def _fnet(u, *, seq_len):
    n, width = u.shape
    nb = n // seq_len
    ch = 128
    n2 = seq_len // DFT_N1
    n1i, k2i, n2i = np.arange(DFT_N1), np.arange(n2), np.arange(n2)
    theta = 2.0 * np.pi * (n1i[:, None, None] * k2i[None, :, None] / seq_len
                           + k2i[None, :, None] * n2i[None, None, :] / n2)
    t1 = np.concatenate([np.cos(theta), -np.sin(theta)], axis=1)
    k1i = np.arange(DFT_N1)
    phi = 2.0 * np.pi * np.outer(k1i, n1i) / DFT_N1
    cg, sg = np.cos(phi), np.sin(phi)
    gmat = np.block([[cg, sg], [-sg, cg]])
    cbd, sbd = _channel_dft_mats(ch)
    cs = np.concatenate([cbd, sbd], axis=0)
    scale = 1.0 / math.sqrt(seq_len * F_GROUP)
    return pl.pallas_call(
        functools.partial(_fnet_kernel, n2=n2, scale=scale),
        grid=(nb, width // ch),
        in_specs=[pl.BlockSpec((seq_len, ch), lambda b, c: (b, c)),
                  _resident(t1.shape), _resident(gmat.shape), _resident(cs.shape)],
        out_specs=pl.BlockSpec((seq_len, ch), lambda b, c: (b, c)),
        out_shape=jax.ShapeDtypeStruct((n, width), F32),
        scratch_shapes=[pltpu.VMEM((n2 * 2 * DFT_N1, ch), F32),
                        pltpu.VMEM(gmat.shape, BF16), pltpu.VMEM(cs.shape, BF16)],
        compiler_params=_cparams(2), name="fnet",
    )(u, jnp.asarray(t1, F32), jnp.asarray(gmat, F32), jnp.asarray(cs, F32))


def _fnet_dense_kernel(x_ref, f_ref, cs_ref, o_ref, *, scale):
    p = _dot(f_ref[...].astype(BF16), x_ref[...].astype(BF16))
    l = p.shape[0] // 2
    pc = jnp.concatenate([p[:l], p[l:]], axis=1).astype(BF16)
    o_ref[...] = _dot(pc, cs_ref[...].astype(BF16)) * scale


def _fnet_dense(u, *, seq_len):
    n, width = u.shape
    pos = np.arange(seq_len)
    ang = 2.0 * np.pi * np.outer(pos, pos) / seq_len
    fmat = np.concatenate([np.cos(ang), np.sin(ang)], axis=0)
    cbd, sbd = _channel_dft_mats(width)
    cs = np.concatenate([cbd, -sbd], axis=0)
    scale = 1.0 / math.sqrt(seq_len * F_GROUP)
    return pl.pallas_call(
        functools.partial(_fnet_dense_kernel, scale=scale),
        grid=(n // seq_len,),
        in_specs=[pl.BlockSpec((seq_len, width), lambda b: (b, 0)),
                  _resident(fmat.shape), _resident(cs.shape)],
        out_specs=pl.BlockSpec((seq_len, width), lambda b: (b, 0)),
        out_shape=jax.ShapeDtypeStruct((n, width), F32),
        compiler_params=_cparams(1), name="fnet_dense",
    )(u, jnp.asarray(fmat, F32), jnp.asarray(cs, F32))


def _masked_queries(qt_ref):
    qt = qt_ref[...]
    row = lax.broadcasted_iota(jnp.int32, qt.shape, 0)
    zero = jnp.zeros_like(qt)
    return jnp.concatenate([jnp.where(row < QK_DIM, qt, zero), jnp.where(row >= QK_DIM, qt, zero)], axis=1)


def _key_chunks(kv_refs, tk):
    chunks, base = [], 0
    for k_ref, vt_ref in kv_refs:
        n = k_ref.shape[0]
        step = min(tk, n)
        for c in range(n // step):
            chunks.append((k_ref, vt_ref, slice(c * step, (c + 1) * step),
                           slice(base + c * step, base + (c + 1) * step)))
        base += n
    return chunks


def _attn_phase(kv_refs, *, tk, q_next_ref=None, s_next_ref=None, m_next_ref=None,
                s_cur_ref=None, m_cur_ref=None):
    qm = _masked_queries(q_next_ref) if q_next_ref is not None else None
    m_cur = m_cur_ref[...].astype(BF16) if s_cur_ref is not None else None
    acc = m_next = None
    for k_ref, vt_ref, src, dst in _key_chunks(kv_refs, tk):
        if s_cur_ref is not None:
            e = jnp.exp2(s_cur_ref[dst, :] - m_cur)
            part = _dot(vt_ref[:, src], e)
            acc = part if acc is None else acc + part
        if q_next_ref is not None:
            s = _dot(k_ref[src, :], qm).astype(BF16)
            s_next_ref[dst, :] = s
            cm = jnp.max(s, axis=0, keepdims=True)
            m_next = cm if m_next is None else jnp.maximum(m_next, cm)
    if q_next_ref is not None:
        m_next_ref[...] = m_next.astype(F32)
    return acc


def _attn_finish(acc, lam, sg_ref, o_ref, *, lam_init):
    tq = acc.shape[1] // 2
    inv = 1.0 / acc[HEAD_W:HEAD_W + 1, :]
    acc = acc[:HEAD_W]
    o = acc[:, :tq] * inv[:, :tq] - lam * (acc[:, tq:] * inv[:, tq:])
    ms = jnp.mean(o * o, axis=0, keepdims=True)
    y = o * lax.rsqrt(ms + EPS) * sg_ref[...] * (1.0 - lam_init)
    o_ref[...] = y.T.astype(BF16)


def _attn_kernel(*refs, n_src, lam_init, tk):
    q0_ref, q1_ref, q2_ref = refs[:3]
    kv_refs = [(refs[3 + 2 * s], refs[4 + 2 * s]) for s in range(n_src)]
    (lq1_ref, lk1_ref, lq2_ref, lk2_ref, sg_ref, o_ref,
     sa_ref, sb_ref, ma_ref, mb_ref) = refs[3 + 2 * n_src:]
    tq = q0_ref.shape[1]
    lam = (jnp.exp(jnp.sum(lq1_ref[...] * lk1_ref[...], axis=1, keepdims=True))
           - jnp.exp(jnp.sum(lq2_ref[...] * lk2_ref[...], axis=1, keepdims=True)) + lam_init)
    phase = functools.partial(_attn_phase, kv_refs, tk=tk)

    @pl.when(pl.program_id(2) == 0)
    def _():
        phase(q_next_ref=q0_ref, s_next_ref=sa_ref, m_next_ref=ma_ref)

    acc = phase(q_next_ref=q1_ref, s_next_ref=sb_ref, m_next_ref=mb_ref, s_cur_ref=sa_ref, m_cur_ref=ma_ref)
    _attn_finish(acc, lam, sg_ref, o_ref.at[0:tq, :], lam_init=lam_init)
    acc = phase(q_next_ref=q2_ref, s_next_ref=sa_ref, m_next_ref=ma_ref, s_cur_ref=sb_ref, m_cur_ref=mb_ref)
    _attn_finish(acc, lam, sg_ref, o_ref.at[tq:2 * tq, :], lam_init=lam_init)


def _attention(qt, kv, lam_vecs, subln_g, *, lq, tq, tk, lam_init):
    n = qt.shape[1]
    nb = n // lq
    nq = lq // tq
    assert nq % 2 == 0
    vec = pl.BlockSpec((1, QK_DIM), lambda b, h, j: (0, 0))
    qspec = lambda off: pl.BlockSpec(
        (HEAD_W, tq), lambda b, h, j: (h, b * nq + jnp.minimum(2 * j + off, nq - 1)))
    kv_specs, kv_args, lk = [], [], 0
    for k, vt in kv:
        ls = k.shape[0] // nb
        assert ls % min(tk, ls) == 0
        kv_specs += [pl.BlockSpec((ls, HEAD_W), lambda b, h, j: (b, h)),
                     pl.BlockSpec((V_ROWS, ls), lambda b, h, j: (h, b))]
        kv_args += [k, vt]
        lk += ls
    return pl.pallas_call(
        functools.partial(_attn_kernel, n_src=len(kv), lam_init=lam_init, tk=tk),
        grid=(nb, N_HEADS, nq // 2),
        in_specs=[qspec(0), qspec(1), qspec(2)] + kv_specs
                 + [vec, vec, vec, vec, pl.BlockSpec((HEAD_W, 1), lambda b, h, j: (0, 0))],
        out_specs=pl.BlockSpec((2 * tq, HEAD_W), lambda b, h, j: (b * (nq // 2) + j, h)),
        out_shape=jax.ShapeDtypeStruct((n, N_HEADS * HEAD_W), BF16),
        scratch_shapes=[pltpu.VMEM((lk, 2 * tq), BF16), pltpu.VMEM((lk, 2 * tq), BF16),
                        pltpu.VMEM((1, 2 * tq), F32), pltpu.VMEM((1, 2 * tq), F32)],
        compiler_params=_cparams(3), name="diff_attn",
    )(qt, qt, qt, *kv_args, *[v.reshape(1, QK_DIM) for v in lam_vecs], subln_g.reshape(HEAD_W, 1))


def _seq_flags(i, tps):
    t = i % tps
    return t == 0, t == tps - 1


def _row_windows(ref, tm):
    sub = 8
    copies = {}

    def window(start):
        r, a = start % sub, start // sub
        assert sub * a + tm <= tm + 2 * HALO - sub
        if r not in copies:
            copies[r] = ref[pl.ds(r, tm + 2 * HALO - sub), :]
        return copies[r][sub * a:sub * a + tm]

    return window


def _local_kernel(u_ref, up_ref, un_ref, dww_ref, dwb_ref, lng_ref, lnb_ref, pw_ref, ps_ref,
                  cc_ref, pp_ref, zc_ref, zp_ref, *, tps, seq_len):
    tm = u_ref.shape[0]
    cw = cc_ref.shape[1]
    first, last = _seq_flags(pl.program_id(0), tps)

    def split(u):
        u = u.astype(F32)
        return u[:, :cw] * jax.nn.sigmoid(u[:, cw:2 * cw]), u[:, 2 * cw:]

    zc, zp = split(u_ref[...])
    zc_ref[HALO:HALO + tm, :] = zc
    zp_ref[HALO:HALO + tm, :] = zp
    zc, zp = split(up_ref[...])
    keep = jnp.where(first, 0.0, 1.0)
    zc_ref[0:HALO, :] = zc * keep
    zp_ref[0:HALO, :] = zp * keep
    zc, zp = split(un_ref[...])
    keep = jnp.where(last, 0.0, 1.0)
    zc_ref[HALO + tm:, :] = zc * keep
    zp_ref[HALO + tm:, :] = zp * keep

    pad = (CONV_TAPS - 1) // 2
    conv_rows = _row_windows(zc_ref, tm)
    acc = jnp.zeros((tm, cw), F32) + dwb_ref[...]
    for r in range(8):
        part = None
        for j in range(CONV_TAPS):
            if (HALO - pad + j) % 8 == r:
                term = conv_rows(HALO - pad + j) * dww_ref[j:j + 1, :]
                part = term if part is None else part + term
        acc = acc + part
    mu = jnp.mean(acc, axis=-1, keepdims=True)
    var = jnp.mean(jnp.square(acc - mu), axis=-1, keepdims=True)
    y = (acc - mu) * lax.rsqrt(var + EPS) * lng_ref[...] + lnb_ref[...]
    cc_ref[...] = (y * jax.nn.sigmoid(y)).astype(BF16)

    pool_rows = _row_windows(zp_ref, tm)

    def shifted(d):
        return pool_rows(HALO + d)

    centre = shifted(0)
    sums = []
    s = None
    prev_half = 0
    for win in POOL_WINDOWS:
        half = win // 2
        for d in list(range(-half, -prev_half)) + list(range(prev_half, half)):
            term = centre if d == 0 else shifted(d)
            s = term if s is None else s + term
        sums.append(s)
        prev_half = half
    pos = (pl.program_id(0) % tps) * tm + lax.broadcasted_iota(jnp.int32, (tm, 1), 0)
    lane_group = lax.broadcasted_iota(jnp.int32, (tm, cw), 1) // POOL_GROUP
    mean = None
    for g, win in enumerate(POOL_WINDOWS):
        half = win // 2
        hi = jnp.minimum(pos + (win - half - 1), seq_len - 1)
        lo = jnp.maximum(pos - half, 0)
        mg = sums[g] / (hi - lo + 1).astype(F32)
        mean = mg if mean is None else jnp.where(lane_group == g, mg, mean)
    d = (mean - centre).astype(BF16)
    pp_ref[...] = (_dot(d, pw_ref[...]) * ps_ref[...]).astype(BF16)


def _local_mix(ucp, dw_w, dw_b, ln_g, ln_b, pool_bd, pool_scale, *, tm, seq_len):
    n, w3 = ucp.shape
    cw = w3 // 3
    tps = seq_len // tm
    r = tm // HALO
    nblk = n // HALO
    row = lambda a: a.reshape(1, cw)
    return pl.pallas_call(
        functools.partial(_local_kernel, tps=tps, seq_len=seq_len),
        grid=(n // tm,),
        in_specs=[pl.BlockSpec((tm, w3), lambda i: (i, 0)),
                  pl.BlockSpec((HALO, w3), lambda i: (jnp.maximum(i * r - 1, 0), 0)),
                  pl.BlockSpec((HALO, w3), lambda i: (jnp.minimum((i + 1) * r, nblk - 1), 0)),
                  _resident((CONV_TAPS, cw)), _resident((1, cw)), _resident((1, cw)), _resident((1, cw)),
                  _resident((cw, cw)), _resident((1, cw))],
        out_specs=(pl.BlockSpec((tm, cw), lambda i: (i, 0)), pl.BlockSpec((tm, cw), lambda i: (i, 0))),
        out_shape=(jax.ShapeDtypeStruct((n, cw), BF16), jax.ShapeDtypeStruct((n, cw), BF16)),
        scratch_shapes=[pltpu.VMEM((tm + 2 * HALO, cw), F32), pltpu.VMEM((tm + 2 * HALO, cw), F32)],
        compiler_params=_cparams(1), name="local_mix",
    )(ucp, ucp, ucp, dw_w, row(dw_b), row(ln_g), row(ln_b), pool_bd, row(pool_scale))


def _merge_kernel(x_ref, mod_ref, g_ref, fm_ref, oa_ref, cc_ref, pp_ref,
                  wg_ref, wof_ref, woa_ref, woc_ref, wop_ref, wout_ref, o_ref):
    d = x_ref.shape[1]
    x = x_ref[...]
    h = _modnorm(x, g_ref[...], mod_ref[0:1, :], mod_ref[1:2, :]).astype(BF16)
    branches = ((fm_ref[...].astype(BF16), wof_ref), (oa_ref[...], woa_ref),
                (cc_ref[...], woc_ref), (pp_ref[...], wop_ref))
    y = None
    for i, (act, w_ref) in enumerate(branches):
        gate = jax.nn.sigmoid(_dot(h, wg_ref[:, i * d:(i + 1) * d]))
        t = gate * _dot(act, w_ref[...])
        y = t if y is None else y + t
    z = _dot(y.astype(BF16), wout_ref[...])
    o_ref[...] = x + mod_ref[2:3, :] * z


def _merge(x, mod, gain, fm, oa, cc, pp, w, *, tm, seq_len):
    n, d = x.shape
    tps = seq_len // tm
    nb = mod.shape[0]
    mod_map = (lambda i: (i // tps, 0, 0)) if nb > 1 else (lambda i: (0, 0, 0))
    tile = lambda a: pl.BlockSpec((tm, a.shape[1]), lambda i: (i, 0))
    return pl.pallas_call(
        _merge_kernel,
        grid=(n // tm,),
        in_specs=[tile(x), pl.BlockSpec((None, 6, d), mod_map), _resident((1, d)),
                  tile(fm), tile(oa), tile(cc), tile(pp)] + [_resident(a.shape) for a in w],
        out_specs=tile(x),
        out_shape=jax.ShapeDtypeStruct((n, d), F32),
        compiler_params=_cparams(1), name="merge",
    )(x, mod, gain.reshape(1, d), fm, oa, cc, pp, *w)


def _ffn_kernel(x_ref, xp_ref, xn_ref, mod_ref, g_ref, wv_ref, wg_ref, dww_ref, dwb_ref, wd_ref, fg_ref,
                o_ref, h_ref, gs_ref, mid_ref, *, tps, final):
    tm = x_ref.shape[0]
    first, last = _seq_flags(pl.program_id(0), tps)
    gain, shift, scale = g_ref[...], mod_ref[3:4, :], mod_ref[4:5, :]
    x = x_ref[...]
    h_ref[HALO:HALO + tm, :] = _modnorm(x, gain, shift, scale).astype(BF16)
    h_ref[0:HALO, :] = _modnorm(xp_ref[...], gain, shift, scale).astype(BF16)
    h_ref[HALO + tm:, :] = _modnorm(xn_ref[...], gain, shift, scale).astype(BF16)
    keep_p = jnp.where(first, 0.0, 1.0)
    keep_n = jnp.where(last, 0.0, 1.0)

    dff = wv_ref.shape[1]
    for c in range(dff // FFN_CHUNK):
        cols = slice(c * FFN_CHUNK, (c + 1) * FFN_CHUNK)
        val = _dot(h_ref[HALO:HALO + tm, :], wv_ref[:, cols])
        gt = _dot(h_ref[...], wg_ref[:, cols])
        gs_ref[HALO:HALO + tm, :] = gt[HALO:HALO + tm]
        gs_ref[0:HALO, :] = gt[:HALO] * keep_p
        gs_ref[HALO + tm:, :] = gt[HALO + tm:] * keep_n
        gc = dwb_ref[:, cols] + gs_ref[pl.ds(HALO, tm), :] * dww_ref[1:2, cols]
        gc = gc + gs_ref[pl.ds(HALO - 1, tm), :] * dww_ref[0:1, cols]
        gc = gc + gs_ref[pl.ds(HALO + 1, tm), :] * dww_ref[2:3, cols]
        act = 0.5 * gc * (1.0 + lax.erf(gc * (1.0 / math.sqrt(2.0))))
        mid_ref[:, cols] = (val * act).astype(BF16)

    y = x + mod_ref[5:6, :] * _dot(mid_ref[...], wd_ref[...])
    if final:
        ms = jnp.mean(y * y, axis=-1, keepdims=True)
        y = y * lax.rsqrt(ms + EPS) * fg_ref[...]
    o_ref[...] = y


def _ffn(x, mod, gain, wv, wg, dw_w, dw_b, wd, final_g, *, tm, seq_len, final):
    n, d = x.shape
    dff = wv.shape[1]
    tps = seq_len // tm
    nb = mod.shape[0]
    r = tm // HALO
    nblk = n // HALO
    mod_map = (lambda i: (i // tps, 0, 0)) if nb > 1 else (lambda i: (0, 0, 0))
    return pl.pallas_call(
        functools.partial(_ffn_kernel, tps=tps, final=final),
        grid=(n // tm,),
        in_specs=[pl.BlockSpec((tm, d), lambda i: (i, 0)),
                  pl.BlockSpec((HALO, d), lambda i: (jnp.maximum(i * r - 1, 0), 0)),
                  pl.BlockSpec((HALO, d), lambda i: (jnp.minimum((i + 1) * r, nblk - 1), 0)),
                  pl.BlockSpec((None, 6, d), mod_map),
                  _resident((1, d)), _resident(wv.shape), _resident(wg.shape),
                  _resident((FFN_TAPS, dff)), _resident((1, dff)), _resident(wd.shape), _resident((1, d))],
        out_specs=pl.BlockSpec((tm, d), lambda i: (i, 0)),
        out_shape=jax.ShapeDtypeStruct((n, d), F32),
        scratch_shapes=[pltpu.VMEM((tm + 2 * HALO, d), BF16),
                        pltpu.VMEM((tm + 2 * HALO, FFN_CHUNK), F32),
                        pltpu.VMEM((tm, dff), BF16)],
        compiler_params=_cparams(1), name="ffn_final" if final else "ffn",
    )(x, x, x, mod, gain.reshape(1, d), wv, wg, dw_w, dw_b.reshape(1, dff), wd, final_g.reshape(1, d))


def _rope_perm():
    q = QK_DIM // 4
    one = np.concatenate([np.arange(0, q), np.arange(2 * q, 3 * q), np.arange(q, 2 * q), np.arange(3 * q, 4 * q)])
    return np.concatenate([g * QK_DIM + one for g in range(2 * N_HEADS)])


def _rope_tables(seq_len):
    nf = QK_DIM // 4
    inv = ROPE_BASE ** (-np.arange(nf, dtype=np.float64) / nf)
    t = np.arange(seq_len)
    ang = np.concatenate([(t // GRID_W)[:, None] * inv, (t % GRID_W)[:, None] * inv], axis=1)
    cos, sin, zero = np.cos(ang), np.sin(ang), np.zeros_like(ang)
    rep = HEAD_W // QK_DIM
    kc = np.tile(np.concatenate([cos, cos], axis=1), (1, rep))
    ksa = np.tile(np.concatenate([zero, sin], axis=1), (1, rep))
    ksb = np.tile(np.concatenate([-sin, zero], axis=1), (1, rep))
    qc = np.concatenate([cos, cos], axis=1).T
    qs = np.concatenate([-sin, sin], axis=1).T
    return tuple(jnp.asarray(a, F32) for a in (kc, ksa, ksb, qc, qs))


def kernel(x, c, ctx, c_ctx, norm1_g, norm2_g, ada_w, ada_b, w_in, lam_q1, lam_k1, lam_q2, lam_k2, subln_g, conv_dw_w, conv_dw_b, conv_ln_g, conv_ln_b, pool_w, pool_scale, wo_f, wo_a, wo_c, wo_p, w_out, w_up, ffn_dw_w, ffn_dw_b, w_down, final_g):
    bsz, seq, d = x.shape
    n_ctx = ctx.shape[1]
    depth = w_in.shape[0]
    d_ff = w_down.shape[1]
    f_w = wo_f.shape[1]
    a_w = wo_a.shape[1]
    c_w = wo_c.shape[1]
    p_w = wo_p.shape[1]
    off_q = f_w
    off_k = off_q + a_w
    off_v = off_k + a_w
    off_c = off_v + a_w
    off_g = off_c + 2 * c_w + p_w
    assert seq % (8 * DFT_N1) == 0 and n_ctx % 256 == 0
    assert a_w == N_HEADS * HEAD_W and d_ff % FFN_CHUNK == 0
    tm_x, tm_c = 512, 256

    cv = jnp.concatenate([c, c_ctx[None, :], jnp.zeros((8 - bsz - 1, d), F32)], axis=0)
    mods = _ada_mod(cv, ada_w, ada_b)
    rope_tabs = _rope_tables(seq)
    perm = _rope_perm()

    xs = x.reshape(bsz * seq, d)
    cs = ctx.reshape(bsz * n_ctx, d)
    out = None
    for l in range(depth):
        last = l == depth - 1
        lam_init = 0.8 - 0.6 * math.exp(-0.3 * l)
        mod_x = mods[l, :bsz].reshape(bsz, 6, d)
        mod_c = mods[l, bsz:bsz + 1].reshape(1, 6, d)
        wl = w_in[l]
        w_a = (wl[:, :off_q].astype(BF16),
               wl[:, off_k:off_v][:, perm].astype(BF16),
               wl[:, off_q:off_k][:, perm].T.astype(BF16),
               wl[:, off_v:off_c].T.astype(BF16),
               wl[:, off_c:off_g].astype(BF16))
        lam_vecs = (lam_q1[l], lam_k1[l], lam_q2[l], lam_k2[l])
        pool_bd = jax.scipy.linalg.block_diag(*[pool_w[l, g] for g in range(len(POOL_WINDOWS))]).astype(BF16)
        w_m = (wl[:, off_g:].astype(BF16), wo_f[l].astype(BF16), wo_a[l].astype(BF16), wo_c[l].astype(BF16),
               wo_p[l].astype(BF16), w_out[l].astype(BF16))
        wv_up = w_up[l][:, :d_ff].astype(BF16)
        wg_up = w_up[l][:, d_ff:].astype(BF16)
        wd = w_down[l].astype(BF16)
        local_w = (conv_dw_w[l], conv_dw_b[l], conv_ln_g[l], conv_ln_b[l], pool_bd, pool_scale[l])

        uf_c, k_c, qt_c, vt_c, ucp_c = _in_proj(cs, mod_c, norm1_g[l], w_a, None, tm=tm_c, seq_len=n_ctx)
        uf_x, k_x, qt_x, vt_x, ucp_x = _in_proj(xs, mod_x, norm1_g[l], w_a, rope_tabs, tm=tm_x, seq_len=seq)
        oa_x = _attention(qt_x, [(k_c, vt_c), (k_x, vt_x)], lam_vecs, subln_g[l],
                          lq=seq, tq=256, tk=ATTN_KEY_CHUNK, lam_init=lam_init)
        fm_x = _fnet(uf_x, seq_len=seq)
        cc_x, pp_x = _local_mix(ucp_x, *local_w, tm=tm_x, seq_len=seq)
        xs = _merge(xs, mod_x, norm1_g[l], fm_x, oa_x, cc_x, pp_x, w_m, tm=tm_x, seq_len=seq)
        xs = _ffn(xs, mod_x, norm2_g[l], wv_up, wg_up, ffn_dw_w[l], ffn_dw_b[l], wd, final_g,
                  tm=tm_x, seq_len=seq, final=last)

        if not last:
            oa_c = _attention(qt_c, [(k_c, vt_c)], lam_vecs, subln_g[l],
                              lq=n_ctx, tq=128, tk=ATTN_KEY_CHUNK, lam_init=lam_init)
            fm_c = _fnet_dense(uf_c, seq_len=n_ctx)
            cc_c, pp_c = _local_mix(ucp_c, *local_w, tm=tm_c, seq_len=n_ctx)
            cs = _merge(cs, mod_c, norm1_g[l], fm_c, oa_c, cc_c, pp_c, w_m, tm=tm_c, seq_len=n_ctx)
            cs = _ffn(cs, mod_c, norm2_g[l], wv_up, wg_up, ffn_dw_w[l], ffn_dw_b[l], wd, final_g,
                      tm=tm_c, seq_len=n_ctx, final=False)
    return xs.reshape(bsz, seq, d)
```

```python
import functools
import math

import numpy as np
import jax
import jax.numpy as jnp
from jax import lax
from jax.experimental import pallas as pl
from jax.experimental.pallas import tpu as pltpu

F32 = jnp.float32
BF16 = jnp.bfloat16

EPS = 1e-6
GRID_W = 64
ROPE_BASE = 10000.0
N_HEADS = 4
QK_DIM = 64
HEAD_W = 2 * QK_DIM
ATT_SCALE = QK_DIM ** -0.5
LOG2E = math.log2(math.e)
ONES_ROWS = 16
V_ROWS = HEAD_W + ONES_ROWS
ROPE_HALF = QK_DIM // 2
CONV_TAPS = 31
POOL_WINDOWS = (2, 4, 8, 16)
POOL_GROUP = 64
F_GROUP = 64
FFN_TAPS = 3
BF16_ROWS = 16
HALO = BF16_ROWS
DFT_N1 = 128
FFN_CHUNK = 256
ATTN_KEY_CHUNK = 512

VMEM_LIMIT_BYTES = 56 * 1024 * 1024


def _cparams(n_axes):
    return pltpu.CompilerParams(dimension_semantics=("arbitrary",) * n_axes,
                                vmem_limit_bytes=VMEM_LIMIT_BYTES)


def _resident(shape):
    nd = len(shape)
    return pl.BlockSpec(shape, lambda *_: (0,) * nd, pipeline_mode=pl.Buffered(1))


def _dot(a, b):
    return jnp.dot(a, b, preferred_element_type=F32)


def _dot_nt(a, b):
    return lax.dot_general(a, b, (((1,), (1,)), ((), ())), preferred_element_type=F32)


def _ada_kernel(cv_ref, w_ref, b_ref, o_ref):
    cv = cv_ref[...]
    s = cv * jax.nn.sigmoid(cv)
    o_ref[...] = jnp.dot(s, w_ref[...], preferred_element_type=F32,
                         precision=lax.Precision.HIGHEST) + b_ref[...]


def _ada_mod(cv, ada_w, ada_b):
    depth, d, d6 = ada_w.shape
    tn = 1536
    return pl.pallas_call(
        _ada_kernel,
        grid=(depth, d6 // tn),
        in_specs=[pl.BlockSpec((8, d), lambda l, j: (0, 0)),
                  pl.BlockSpec((None, d, tn), lambda l, j: (l, 0, j)),
                  pl.BlockSpec((None, 1, tn), lambda l, j: (l, 0, j))],
        out_specs=pl.BlockSpec((None, 8, tn), lambda l, j: (l, 0, j)),
        out_shape=jax.ShapeDtypeStruct((depth, 8, d6), F32),
        compiler_params=_cparams(2),
        name="ada_mod",
    )(cv, ada_w, ada_b.reshape(depth, 1, d6))


def _modnorm(x, gain, shift, scale):
    ms = jnp.mean(x * x, axis=-1, keepdims=True)
    return x * lax.rsqrt(ms + EPS) * gain * (1.0 + scale) + shift


def _in_proj_kernel(*refs, rope):
    if rope:
        (x_ref, mod_ref, g_ref, wf_ref, wk_ref, wqt_ref, wvt_ref, wcp_ref,
         kc_ref, ksa_ref, ksb_ref, qc_ref, qs_ref,
         uf_ref, k_ref, qt_ref, vt_ref, ucp_ref) = refs
    else:
        (x_ref, mod_ref, g_ref, wf_ref, wk_ref, wqt_ref, wvt_ref, wcp_ref,
         uf_ref, k_ref, qt_ref, vt_ref, ucp_ref) = refs
    h = _modnorm(x_ref[...], g_ref[...], mod_ref[0:1, :], mod_ref[1:2, :]).astype(BF16)
    tm = h.shape[0]

    uf_ref[...] = _dot(h, wf_ref[...])
    ucp_ref[...] = _dot(h, wcp_ref[...]).astype(BF16)
    vt = _dot_nt(wvt_ref[...], h).astype(BF16)
    for hd in range(N_HEADS):
        vt_ref[hd * V_ROWS:hd * V_ROWS + HEAD_W, :] = vt[hd * HEAD_W:(hd + 1) * HEAD_W]
        vt_ref[hd * V_ROWS + HEAD_W:(hd + 1) * V_ROWS, :] = jnp.ones((ONES_ROWS, tm), BF16)

    k = _dot(h, wk_ref[...])
    qt = _dot_nt(wqt_ref[...], h)
    if rope:
        kc, ksa, ksb = kc_ref[...], ksa_ref[...], ksb_ref[...]
        for hd in range(N_HEADS):
            kh = k[:, hd * HEAD_W:(hd + 1) * HEAD_W]
            kh = kh * kc + pltpu.roll(kh, HEAD_W - ROPE_HALF, 1) * ksb + pltpu.roll(kh, ROPE_HALF, 1) * ksa
            k_ref[:, hd * HEAD_W:(hd + 1) * HEAD_W] = kh.astype(BF16)
        q3 = qt.reshape(2 * N_HEADS, QK_DIM, tm)
        q3s = jnp.concatenate([q3[:, ROPE_HALF:, :], q3[:, :ROPE_HALF, :]], axis=1)
        q3 = q3 * qc_ref[...][None] + q3s * qs_ref[...][None]
        qt = q3.reshape(2 * N_HEADS * QK_DIM, tm)
    else:
        k_ref[...] = k.astype(BF16)
    qt_ref[...] = (qt * (ATT_SCALE * LOG2E)).astype(BF16)


def _in_proj(x, mod, gain, w, rope_tabs, *, tm, seq_len):
    n, d = x.shape
    tps = seq_len // tm
    nb = mod.shape[0]
    rope = rope_tabs is not None
    mod_map = (lambda i: (i // tps, 0, 0)) if nb > 1 else (lambda i: (0, 0, 0))
    in_specs = [pl.BlockSpec((tm, d), lambda i: (i, 0)),
                pl.BlockSpec((None, 6, d), mod_map),
                _resident((1, d))] + [_resident(a.shape) for a in w]
    args = [x, mod, gain.reshape(1, d)] + list(w)
    if rope:
        kc, ksa, ksb, qc, qs = rope_tabs
        in_specs += [pl.BlockSpec((tm, HEAD_W), lambda i: (i % tps, 0))] * 3
        in_specs += [pl.BlockSpec((QK_DIM, tm), lambda i: (0, i % tps))] * 2
        args += [kc, ksa, ksb, qc, qs]
    wf, wk, wqt, wvt, wcp = w
    out_shape = (jax.ShapeDtypeStruct((n, wf.shape[1]), F32),
                 jax.ShapeDtypeStruct((n, wk.shape[1]), BF16),
                 jax.ShapeDtypeStruct((wqt.shape[0], n), BF16),
                 jax.ShapeDtypeStruct((N_HEADS * V_ROWS, n), BF16),
                 jax.ShapeDtypeStruct((n, wcp.shape[1]), BF16))
    out_specs = (pl.BlockSpec((tm, wf.shape[1]), lambda i: (i, 0)),
                 pl.BlockSpec((tm, wk.shape[1]), lambda i: (i, 0)),
                 pl.BlockSpec((wqt.shape[0], tm), lambda i: (0, i)),
                 pl.BlockSpec((N_HEADS * V_ROWS, tm), lambda i: (0, i)),
                 pl.BlockSpec((tm, wcp.shape[1]), lambda i: (i, 0)))
    return pl.pallas_call(
        functools.partial(_in_proj_kernel, rope=rope),
        grid=(n // tm,), in_specs=in_specs, out_specs=out_specs, out_shape=out_shape,
        compiler_params=_cparams(1), name="in_proj_rope" if rope else "in_proj",
    )(*args)


def _fnet_kernel(x_ref, t1_ref, g32_ref, cs32_ref, o_ref, zz_ref, g_ref, cs_ref, *, n2, scale):
    g_ref[...] = g32_ref[...].astype(BF16)
    cs_ref[...] = cs32_ref[...].astype(BF16)

    def stage1(n1, carry):
        xs = x_ref[pl.ds(n1, n2, stride=DFT_N1), :].astype(BF16)
        z = _dot(t1_ref[n1].astype(BF16), xs)
        zz_ref[pl.ds(n1, n2, stride=2 * DFT_N1), :] = z[:n2]
        zz_ref[pl.ds(DFT_N1 + n1, n2, stride=2 * DFT_N1), :] = z[n2:]
        return carry

    lax.fori_loop(0, DFT_N1, stage1, 0, unroll=8)

    def stage2(k2, carry):
        base = pl.multiple_of(k2 * (2 * DFT_N1), 2 * DFT_N1)
        zk = zz_ref[pl.ds(base, 2 * DFT_N1), :].astype(BF16)
        p = _dot(g_ref[...], zk)
        pc = jnp.concatenate([p[:DFT_N1], p[DFT_N1:]], axis=1).astype(BF16)
        y = _dot(pc, cs_ref[...]) * scale
        o_ref[pl.ds(k2, DFT_N1, stride=n2), :] = y
        return carry

    lax.fori_loop(0, n2, stage2, 0, unroll=8)


def _channel_dft_mats(ch):
    c = np.arange(F_GROUP)
    ang = 2.0 * np.pi * np.outer(c, c) / F_GROUP
    eye = np.eye(ch // F_GROUP)
    return np.kron(eye, np.cos(ang)), np.kron(eye, np.sin(ang))


def _fnet(u, *, seq_len):
    n, width = u.shape
    nb = n // seq_len
    ch = 128
    n2 = seq_len // DFT_N1
    n1i, k2i, n2i = np.arange(DFT_N1), np.arange(n2), np.arange(n2)
    theta = 2.0 * np.pi * (n1i[:, None, None] * k2i[None, :, None] / seq_len
                           + k2i[None, :, None] * n2i[None, None, :] / n2)
    t1 = np.concatenate([np.cos(theta), -np.sin(theta)], axis=1)
    k1i = np.arange(DFT_N1)
    phi = 2.0 * np.pi * np.outer(k1i, n1i) / DFT_N1
    cg, sg = np.cos(phi), np.sin(phi)
    gmat = np.block([[cg, sg], [-sg, cg]])
    cbd, sbd = _channel_dft_mats(ch)
    cs = np.concatenate([cbd, sbd], axis=0)
    scale = 1.0 / math.sqrt(seq_len * F_GROUP)
    return pl.pallas_call(
        functools.partial(_fnet_kernel, n2=n2, scale=scale),
        grid=(nb, width // ch),
        in_specs=[pl.BlockSpec((seq_len, ch), lambda b, c: (b, c)),
                  _resident(t1.shape), _resident(gmat.shape), _resident(cs.shape)],
        out_specs=pl.BlockSpec((seq_len, ch), lambda b, c: (b, c)),
        out_shape=jax.ShapeDtypeStruct((n, width), F32),
        scratch_shapes=[pltpu.VMEM((n2 * 2 * DFT_N1, ch), F32),
                        pltpu.VMEM(gmat.shape, BF16), pltpu.VMEM(cs.shape, BF16)],
        compiler_params=_cparams(2), name="fnet",
    )(u, jnp.asarray(t1, F32), jnp.asarray(gmat, F32), jnp.asarray(cs, F32))


def _fnet_dense_kernel(x_ref, f_ref, cs_ref, o_ref, *, scale):
    p = _dot(f_ref[...].astype(BF16), x_ref[...].astype(BF16))
    l = p.shape[0] // 2
    pc = jnp.concatenate([p[:l], p[l:]], axis=1).astype(BF16)
    o_ref[...] = _dot(pc, cs_ref[...].astype(BF16)) * scale


def _fnet_dense(u, *, seq_len):
    n, width = u.shape
    pos = np.arange(seq_len)
    ang = 2.0 * np.pi * np.outer(pos, pos) / seq_len
    fmat = np.concatenate([np.cos(ang), np.sin(ang)], axis=0)
    cbd, sbd = _channel_dft_mats(width)
    cs = np.concatenate([cbd, -sbd], axis=0)
    scale = 1.0 / math.sqrt(seq_len * F_GROUP)
    return pl.pallas_call(
        functools.partial(_fnet_dense_kernel, scale=scale),
        grid=(n // seq_len,),
        in_specs=[pl.BlockSpec((seq_len, width), lambda b: (b, 0)),
                  _resident(fmat.shape), _resident(cs.shape)],
        out_specs=pl.BlockSpec((seq_len, width), lambda b: (b, 0)),
        out_shape=jax.ShapeDtypeStruct((n, width), F32),
        compiler_params=_cparams(1), name="fnet_dense",
    )(u, jnp.asarray(fmat, F32), jnp.asarray(cs, F32))


def _masked_queries(qt_ref):
    qt = qt_ref[...]
    row = lax.broadcasted_iota(jnp.int32, qt.shape, 0)
    zero = jnp.zeros_like(qt)
    return jnp.concatenate([jnp.where(row < QK_DIM, qt, zero), jnp.where(row >= QK_DIM, qt, zero)], axis=1)


def _key_chunks(kv_refs, tk):
    chunks = []
    for k_ref, vt_ref in kv_refs:
        n = k_ref.shape[0]
        step = min(tk, n)
        chunks += [(k_ref, vt_ref, slice(c * step, (c + 1) * step)) for c in range(n // step)]
    return chunks


PV_LAG = 2


def _attn_tiles(qm, kv_refs, *, tk):
    n_t = len(qm)
    m = [jnp.full((1, qm[0].shape[1]), -jnp.inf, F32) for _ in range(n_t)]
    acc = [jnp.zeros((V_ROWS, qm[0].shape[1]), F32) for _ in range(n_t)]
    pending = []

    def value_matmul(t, vt_ref, rows, e, alpha):
        acc[t] = _dot(vt_ref[:, rows], e) + acc[t] * alpha

    for k_ref, vt_ref, rows in _key_chunks(kv_refs, tk):
        for t in range(n_t):
            s = _dot(k_ref[rows, :], qm[t]).astype(BF16)
            if len(pending) >= PV_LAG:
                value_matmul(*pending.pop(0))
            m_new = jnp.maximum(m[t], jnp.max(s, axis=0, keepdims=True).astype(F32))
            alpha = jnp.exp2(m[t] - m_new)
            e = jnp.exp2(s - m_new.astype(BF16))
            m[t] = m_new
            pending.append((t, vt_ref, rows, e, alpha))
    for p in pending:
        value_matmul(*p)
    return acc


def _attn_finish(acc, lam, sg_ref, o_ref, *, lam_init):
    tq = acc.shape[1] // 2
    inv = 1.0 / acc[HEAD_W:HEAD_W + 1, :]
    acc = acc[:HEAD_W]
    o = acc[:, :tq] * inv[:, :tq] - lam * (acc[:, tq:] * inv[:, tq:])
    ms = jnp.mean(o * o, axis=0, keepdims=True)
    y = o * lax.rsqrt(ms + EPS) * sg_ref[...] * (1.0 - lam_init)
    o_ref[...] = y.T.astype(BF16)


def _attn_kernel(*refs, n_src, n_tiles, lam_init, tk):
    q_ref = refs[0]
    kv_refs = [(refs[1 + 2 * s], refs[2 + 2 * s]) for s in range(n_src)]
    lq1_ref, lk1_ref, lq2_ref, lk2_ref, sg_ref, o_ref = refs[1 + 2 * n_src:]
    tq = q_ref.shape[1] // n_tiles
    lam = (jnp.exp(jnp.sum(lq1_ref[...] * lk1_ref[...], axis=1, keepdims=True))
           - jnp.exp(jnp.sum(lq2_ref[...] * lk2_ref[...], axis=1, keepdims=True)) + lam_init)
    qm = [_masked_queries(q_ref.at[:, t * tq:(t + 1) * tq]) for t in range(n_tiles)]
    acc = _attn_tiles(qm, kv_refs, tk=tk)
    for t in range(n_tiles):
        _attn_finish(acc[t], lam, sg_ref, o_ref.at[t * tq:(t + 1) * tq, :], lam_init=lam_init)


def _attention(qt, kv, lam_vecs, subln_g, *, lq, tq, n_tiles, tk, lam_init):
    n = qt.shape[1]
    nb = n // lq
    steps = lq // (n_tiles * tq)
    assert lq % (n_tiles * tq) == 0
    vec = pl.BlockSpec((1, QK_DIM), lambda b, h, j: (0, 0))
    kv_specs, kv_args = [], []
    for k, vt in kv:
        ls = k.shape[0] // nb
        assert ls % min(tk, ls) == 0
        kv_specs += [pl.BlockSpec((ls, HEAD_W), lambda b, h, j: (b, h)),
                     pl.BlockSpec((V_ROWS, ls), lambda b, h, j: (h, b))]
        kv_args += [k, vt]
    return pl.pallas_call(
        functools.partial(_attn_kernel, n_src=len(kv), n_tiles=n_tiles, lam_init=lam_init, tk=tk),
        grid=(nb, N_HEADS, steps),
        in_specs=[pl.BlockSpec((HEAD_W, n_tiles * tq), lambda b, h, j: (h, b * steps + j))] + kv_specs
                 + [vec, vec, vec, vec, pl.BlockSpec((HEAD_W, 1), lambda b, h, j: (0, 0))],
        out_specs=pl.BlockSpec((n_tiles * tq, HEAD_W), lambda b, h, j: (b * steps + j, h)),
        out_shape=jax.ShapeDtypeStruct((n, N_HEADS * HEAD_W), BF16),
        compiler_params=_cparams(3), name="diff_attn",
    )(qt, *kv_args, *[v.reshape(1, QK_DIM) for v in lam_vecs], subln_g.reshape(HEAD_W, 1))


def _seq_flags(i, tps):
    t = i % tps
    return t == 0, t == tps - 1


def _row_windows(ref, tm):
    sub = 8
    copies = {}

    def window(start):
        r, a = start % sub, start // sub
        assert sub * a + tm <= tm + 2 * HALO - sub
        if r not in copies:
            copies[r] = ref[pl.ds(r, tm + 2 * HALO - sub), :]
        return copies[r][sub * a:sub * a + tm]

    return window


def _local_kernel(u_ref, up_ref, un_ref, dww_ref, dwb_ref, lng_ref, lnb_ref, pw_ref, ps_ref,
                  cc_ref, pp_ref, zc_ref, zp_ref, *, tps, seq_len):
    tm = u_ref.shape[0]
    cw = cc_ref.shape[1]
    first, last = _seq_flags(pl.program_id(0), tps)

    def split(u):
        u = u.astype(F32)
        return u[:, :cw] * jax.nn.sigmoid(u[:, cw:2 * cw]), u[:, 2 * cw:]

    zc, zp = split(u_ref[...])
    zc_ref[HALO:HALO + tm, :] = zc
    zp_ref[HALO:HALO + tm, :] = zp
    zc, zp = split(up_ref[...])
    keep = jnp.where(first, 0.0, 1.0)
    zc_ref[0:HALO, :] = zc * keep
    zp_ref[0:HALO, :] = zp * keep
    zc, zp = split(un_ref[...])
    keep = jnp.where(last, 0.0, 1.0)
    zc_ref[HALO + tm:, :] = zc * keep
    zp_ref[HALO + tm:, :] = zp * keep

    pad = (CONV_TAPS - 1) // 2
    conv_rows = _row_windows(zc_ref, tm)
    acc = jnp.zeros((tm, cw), F32) + dwb_ref[...]
    for r in range(8):
        part = None
        for j in range(CONV_TAPS):
            if (HALO - pad + j) % 8 == r:
                term = conv_rows(HALO - pad + j) * dww_ref[j:j + 1, :]
                part = term if part is None else part + term
        acc = acc + part
    mu = jnp.mean(acc, axis=-1, keepdims=True)
    var = jnp.mean(jnp.square(acc - mu), axis=-1, keepdims=True)
    y = (acc - mu) * lax.rsqrt(var + EPS) * lng_ref[...] + lnb_ref[...]
    cc_ref[...] = (y * jax.nn.sigmoid(y)).astype(BF16)

    pool_rows = _row_windows(zp_ref, tm)

    def shifted(d):
        return pool_rows(HALO + d)

    centre = shifted(0)
    sums = []
    s = None
    prev_half = 0
    for win in POOL_WINDOWS:
        half = win // 2
        for d in list(range(-half, -prev_half)) + list(range(prev_half, half)):
            term = centre if d == 0 else shifted(d)
            s = term if s is None else s + term
        sums.append(s)
        prev_half = half
    pos = (pl.program_id(0) % tps) * tm + lax.broadcasted_iota(jnp.int32, (tm, 1), 0)
    lane_group = lax.broadcasted_iota(jnp.int32, (tm, cw), 1) // POOL_GROUP
    mean = None
    for g, win in enumerate(POOL_WINDOWS):
        half = win // 2
        hi = jnp.minimum(pos + (win - half - 1), seq_len - 1)
        lo = jnp.maximum(pos - half, 0)
        mg = sums[g] / (hi - lo + 1).astype(F32)
        mean = mg if mean is None else jnp.where(lane_group == g, mg, mean)
    d = (mean - centre).astype(BF16)
    pp_ref[...] = (_dot(d, pw_ref[...]) * ps_ref[...]).astype(BF16)


def _local_mix(ucp, dw_w, dw_b, ln_g, ln_b, pool_bd, pool_scale, *, tm, seq_len):
    n, w3 = ucp.shape
    cw = w3 // 3
    tps = seq_len // tm
    r = tm // HALO
    nblk = n // HALO
    row = lambda a: a.reshape(1, cw)
    return pl.pallas_call(
        functools.partial(_local_kernel, tps=tps, seq_len=seq_len),
        grid=(n // tm,),
        in_specs=[pl.BlockSpec((tm, w3), lambda i: (i, 0)),
                  pl.BlockSpec((HALO, w3), lambda i: (jnp.maximum(i * r - 1, 0), 0)),
                  pl.BlockSpec((HALO, w3), lambda i: (jnp.minimum((i + 1) * r, nblk - 1), 0)),
                  _resident((CONV_TAPS, cw)), _resident((1, cw)), _resident((1, cw)), _resident((1, cw)),
                  _resident((cw, cw)), _resident((1, cw))],
        out_specs=(pl.BlockSpec((tm, cw), lambda i: (i, 0)), pl.BlockSpec((tm, cw), lambda i: (i, 0))),
        out_shape=(jax.ShapeDtypeStruct((n, cw), BF16), jax.ShapeDtypeStruct((n, cw), BF16)),
        scratch_shapes=[pltpu.VMEM((tm + 2 * HALO, cw), F32), pltpu.VMEM((tm + 2 * HALO, cw), F32)],
        compiler_params=_cparams(1), name="local_mix",
    )(ucp, ucp, ucp, dw_w, row(dw_b), row(ln_g), row(ln_b), pool_bd, row(pool_scale))


def _merge_kernel(x_ref, mod_ref, g_ref, fm_ref, oa_ref, cc_ref, pp_ref,
                  wg_ref, wof_ref, woa_ref, woc_ref, wop_ref, wout_ref, o_ref):
    d = x_ref.shape[1]
    x = x_ref[...]
    h = _modnorm(x, g_ref[...], mod_ref[0:1, :], mod_ref[1:2, :]).astype(BF16)
    branches = ((fm_ref[...].astype(BF16), wof_ref), (oa_ref[...], woa_ref),
                (cc_ref[...], woc_ref), (pp_ref[...], wop_ref))
    y = None
    for i, (act, w_ref) in enumerate(branches):
        gate = jax.nn.sigmoid(_dot(h, wg_ref[:, i * d:(i + 1) * d]))
        t = gate * _dot(act, w_ref[...])
        y = t if y is None else y + t
    z = _dot(y.astype(BF16), wout_ref[...])
    o_ref[...] = x + mod_ref[2:3, :] * z


def _merge(x, mod, gain, fm, oa, cc, pp, w, *, tm, seq_len):
    n, d = x.shape
    tps = seq_len // tm
    nb = mod.shape[0]
    mod_map = (lambda i: (i // tps, 0, 0)) if nb > 1 else (lambda i: (0, 0, 0))
    tile = lambda a: pl.BlockSpec((tm, a.shape[1]), lambda i: (i, 0))
    return pl.pallas_call(
        _merge_kernel,
        grid=(n // tm,),
        in_specs=[tile(x), pl.BlockSpec((None, 6, d), mod_map), _resident((1, d)),
                  tile(fm), tile(oa), tile(cc), tile(pp)] + [_resident(a.shape) for a in w],
        out_specs=tile(x),
        out_shape=jax.ShapeDtypeStruct((n, d), F32),
        compiler_params=_cparams(1), name="merge",
    )(x, mod, gain.reshape(1, d), fm, oa, cc, pp, *w)


def _ffn_kernel(x_ref, xp_ref, xn_ref, mod_ref, g_ref, wv_ref, wg_ref, dww_ref, dwb_ref, wd_ref, fg_ref,
                o_ref, h_ref, gs_ref, mid_ref, *, tps, final):
    tm = x_ref.shape[0]
    first, last = _seq_flags(pl.program_id(0), tps)
    gain, shift, scale = g_ref[...], mod_ref[3:4, :], mod_ref[4:5, :]
    x = x_ref[...]
    h_ref[HALO:HALO + tm, :] = _modnorm(x, gain, shift, scale).astype(BF16)
    h_ref[0:HALO, :] = _modnorm(xp_ref[...], gain, shift, scale).astype(BF16)
    h_ref[HALO + tm:, :] = _modnorm(xn_ref[...], gain, shift, scale).astype(BF16)
    keep_p = jnp.where(first, 0.0, 1.0)
    keep_n = jnp.where(last, 0.0, 1.0)

    dff = wv_ref.shape[1]
    for c in range(dff // FFN_CHUNK):
        cols = slice(c * FFN_CHUNK, (c + 1) * FFN_CHUNK)
        val = _dot(h_ref[HALO:HALO + tm, :], wv_ref[:, cols])
        gt = _dot(h_ref[...], wg_ref[:, cols])
        gs_ref[HALO:HALO + tm, :] = gt[HALO:HALO + tm]
        gs_ref[0:HALO, :] = gt[:HALO] * keep_p
        gs_ref[HALO + tm:, :] = gt[HALO + tm:] * keep_n
        gc = dwb_ref[:, cols] + gs_ref[pl.ds(HALO, tm), :] * dww_ref[1:2, cols]
        gc = gc + gs_ref[pl.ds(HALO - 1, tm), :] * dww_ref[0:1, cols]
        gc = gc + gs_ref[pl.ds(HALO + 1, tm), :] * dww_ref[2:3, cols]
        act = 0.5 * gc * (1.0 + lax.erf(gc * (1.0 / math.sqrt(2.0))))
        mid_ref[:, cols] = (val * act).astype(BF16)

    y = x + mod_ref[5:6, :] * _dot(mid_ref[...], wd_ref[...])
    if final:
        ms = jnp.mean(y * y, axis=-1, keepdims=True)
        y = y * lax.rsqrt(ms + EPS) * fg_ref[...]
    o_ref[...] = y


def _ffn(x, mod, gain, wv, wg, dw_w, dw_b, wd, final_g, *, tm, seq_len, final):
    n, d = x.shape
    dff = wv.shape[1]
    tps = seq_len // tm
    nb = mod.shape[0]
    r = tm // HALO
    nblk = n // HALO
    mod_map = (lambda i: (i // tps, 0, 0)) if nb > 1 else (lambda i: (0, 0, 0))
    return pl.pallas_call(
        functools.partial(_ffn_kernel, tps=tps, final=final),
        grid=(n // tm,),
        in_specs=[pl.BlockSpec((tm, d), lambda i: (i, 0)),
                  pl.BlockSpec((HALO, d), lambda i: (jnp.maximum(i * r - 1, 0), 0)),
                  pl.BlockSpec((HALO, d), lambda i: (jnp.minimum((i + 1) * r, nblk - 1), 0)),
                  pl.BlockSpec((None, 6, d), mod_map),
                  _resident((1, d)), _resident(wv.shape), _resident(wg.shape),
                  _resident((FFN_TAPS, dff)), _resident((1, dff)), _resident(wd.shape), _resident((1, d))],
        out_specs=pl.BlockSpec((tm, d), lambda i: (i, 0)),
        out_shape=jax.ShapeDtypeStruct((n, d), F32),
        scratch_shapes=[pltpu.VMEM((tm + 2 * HALO, d), BF16),
                        pltpu.VMEM((tm + 2 * HALO, FFN_CHUNK), F32),
                        pltpu.VMEM((tm, dff), BF16)],
        compiler_params=_cparams(1), name="ffn_final" if final else "ffn",
    )(x, x, x, mod, gain.reshape(1, d), wv, wg, dw_w, dw_b.reshape(1, dff), wd, final_g.reshape(1, d))


def _rope_perm():
    q = QK_DIM // 4
    one = np.concatenate([np.arange(0, q), np.arange(2 * q, 3 * q), np.arange(q, 2 * q), np.arange(3 * q, 4 * q)])
    return np.concatenate([g * QK_DIM + one for g in range(2 * N_HEADS)])


def _rope_tables(seq_len):
    nf = QK_DIM // 4
    inv = ROPE_BASE ** (-np.arange(nf, dtype=np.float64) / nf)
    t = np.arange(seq_len)
    ang = np.concatenate([(t // GRID_W)[:, None] * inv, (t % GRID_W)[:, None] * inv], axis=1)
    cos, sin, zero = np.cos(ang), np.sin(ang), np.zeros_like(ang)
    rep = HEAD_W // QK_DIM
    kc = np.tile(np.concatenate([cos, cos], axis=1), (1, rep))
    ksa = np.tile(np.concatenate([zero, sin], axis=1), (1, rep))
    ksb = np.tile(np.concatenate([-sin, zero], axis=1), (1, rep))
    qc = np.concatenate([cos, cos], axis=1).T
    qs = np.concatenate([-sin, sin], axis=1).T
    return tuple(jnp.asarray(a, F32) for a in (kc, ksa, ksb, qc, qs))


def kernel(x, c, ctx, c_ctx, norm1_g, norm2_g, ada_w, ada_b, w_in, lam_q1, lam_k1, lam_q2, lam_k2, subln_g, conv_dw_w, conv_dw_b, conv_ln_g, conv_ln_b, pool_w, pool_scale, wo_f, wo_a, wo_c, wo_p, w_out, w_up, ffn_dw_w, ffn_dw_b, w_down, final_g):
    bsz, seq, d = x.shape
    n_ctx = ctx.shape[1]
    depth = w_in.shape[0]
    d_ff = w_down.shape[1]
    f_w = wo_f.shape[1]
    a_w = wo_a.shape[1]
    c_w = wo_c.shape[1]
    p_w = wo_p.shape[1]
    off_q = f_w
    off_k = off_q + a_w
    off_v = off_k + a_w
    off_c = off_v + a_w
    off_g = off_c + 2 * c_w + p_w
    assert seq % (8 * DFT_N1) == 0 and n_ctx % 256 == 0
    assert a_w == N_HEADS * HEAD_W and d_ff % FFN_CHUNK == 0
    tm_x, tm_c = 512, 256

    cv = jnp.concatenate([c, c_ctx[None, :], jnp.zeros((8 - bsz - 1, d), F32)], axis=0)
    mods = _ada_mod(cv, ada_w, ada_b)
    rope_tabs = _rope_tables(seq)
    perm = _rope_perm()

    xs = x.reshape(bsz * seq, d)
    cs = ctx.reshape(bsz * n_ctx, d)
    out = None
    for l in range(depth):
        last = l == depth - 1
        lam_init = 0.8 - 0.6 * math.exp(-0.3 * l)
        mod_x = mods[l, :bsz].reshape(bsz, 6, d)
        mod_c = mods[l, bsz:bsz + 1].reshape(1, 6, d)
        wl = w_in[l]
        w_a = (wl[:, :off_q].astype(BF16),
               wl[:, off_k:off_v][:, perm].astype(BF16),
               wl[:, off_q:off_k][:, perm].T.astype(BF16),
               wl[:, off_v:off_c].T.astype(BF16),
               wl[:, off_c:off_g].astype(BF16))
        lam_vecs = (lam_q1[l], lam_k1[l], lam_q2[l], lam_k2[l])
        pool_bd = jax.scipy.linalg.block_diag(*[pool_w[l, g] for g in range(len(POOL_WINDOWS))]).astype(BF16)
        w_m = (wl[:, off_g:].astype(BF16), wo_f[l].astype(BF16), wo_a[l].astype(BF16), wo_c[l].astype(BF16),
               wo_p[l].astype(BF16), w_out[l].astype(BF16))
        wv_up = w_up[l][:, :d_ff].astype(BF16)
        wg_up = w_up[l][:, d_ff:].astype(BF16)
        wd = w_down[l].astype(BF16)
        local_w = (conv_dw_w[l], conv_dw_b[l], conv_ln_g[l], conv_ln_b[l], pool_bd, pool_scale[l])

        uf_c, k_c, qt_c, vt_c, ucp_c = _in_proj(cs, mod_c, norm1_g[l], w_a, None, tm=tm_c, seq_len=n_ctx)
        uf_x, k_x, qt_x, vt_x, ucp_x = _in_proj(xs, mod_x, norm1_g[l], w_a, rope_tabs, tm=tm_x, seq_len=seq)
        oa_x = _attention(qt_x, [(k_c, vt_c), (k_x, vt_x)], lam_vecs, subln_g[l],
                          lq=seq, tq=256, n_tiles=4, tk=ATTN_KEY_CHUNK, lam_init=lam_init)
        fm_x = _fnet(uf_x, seq_len=seq)
        cc_x, pp_x = _local_mix(ucp_x, *local_w, tm=tm_x, seq_len=seq)
        xs = _merge(xs, mod_x, norm1_g[l], fm_x, oa_x, cc_x, pp_x, w_m, tm=tm_x, seq_len=seq)
        xs = _ffn(xs, mod_x, norm2_g[l], wv_up, wg_up, ffn_dw_w[l], ffn_dw_b[l], wd, final_g,
                  tm=tm_x, seq_len=seq, final=last)

        if not last:
            oa_c = _attention(qt_c, [(k_c, vt_c)], lam_vecs, subln_g[l],
                              lq=n_ctx, tq=128, n_tiles=2, tk=ATTN_KEY_CHUNK, lam_init=lam_init)
            fm_c = _fnet_dense(uf_c, seq_len=n_ctx)
            cc_c, pp_c = _local_mix(ucp_c, *local_w, tm=tm_c, seq_len=n_ctx)
            cs = _merge(cs, mod_c, norm1_g[l], fm_c, oa_c, cc_c, pp_c, w_m, tm=tm_c, seq_len=n_ctx)
            cs = _ffn(cs, mod_c, norm2_g[l], wv_up, wg_up, ffn_dw_w[l], ffn_dw_b[l], wd, final_g,
                      tm=tm_c, seq_len=n_ctx, final=False)
    return xs.reshape(bsz, seq, d)
```

```python
import functools
import math

import numpy as np
import jax
import jax.numpy as jnp
from jax import lax
from jax.experimental import pallas as pl
from jax.experimental.pallas import tpu as pltpu

F32 = jnp.float32
BF16 = jnp.bfloat16

EPS = 1e-6
GRID_W = 64
ROPE_BASE = 10000.0
N_HEADS = 4
QK_DIM = 64
HEAD_W = 2 * QK_DIM
ATT_SCALE = QK_DIM ** -0.5
LOG2E = math.log2(math.e)
ONES_ROWS = 16
V_ROWS = HEAD_W + ONES_ROWS
ROPE_HALF = QK_DIM // 2
CONV_TAPS = 31
POOL_WINDOWS = (2, 4, 8, 16)
POOL_GROUP = 64
F_GROUP = 64
FFN_TAPS = 3
BF16_ROWS = 16
HALO = BF16_ROWS
DFT_N1 = 128
PITCH_PAD = 8
FFN_CHUNK = 256
ATTN_KEY_CHUNK = 512

VMEM_LIMIT_BYTES = 56 * 1024 * 1024


def _cparams(n_axes):
    return pltpu.CompilerParams(dimension_semantics=("arbitrary",) * n_axes,
                                vmem_limit_bytes=VMEM_LIMIT_BYTES)


def _resident(shape):
    nd = len(shape)
    return pl.BlockSpec(shape, lambda *_: (0,) * nd, pipeline_mode=pl.Buffered(1))


def _dot(a, b):
    return jnp.dot(a, b, preferred_element_type=F32)


def _dot_nt(a, b):
    return lax.dot_general(a, b, (((1,), (1,)), ((), ())), preferred_element_type=F32)


def _ada_kernel(cv_ref, w_ref, b_ref, o_ref):
    cv = cv_ref[...]
    s = cv * jax.nn.sigmoid(cv)
    o_ref[...] = jnp.dot(s, w_ref[...], preferred_element_type=F32,
                         precision=lax.Precision.HIGHEST) + b_ref[...]


def _ada_mod(cv, ada_w, ada_b):
    depth, d, d6 = ada_w.shape
    tn = 1536
    return pl.pallas_call(
        _ada_kernel,
        grid=(depth, d6 // tn),
        in_specs=[pl.BlockSpec((8, d), lambda l, j: (0, 0)),
                  pl.BlockSpec((None, d, tn), lambda l, j: (l, 0, j)),
                  pl.BlockSpec((None, 1, tn), lambda l, j: (l, 0, j))],
        out_specs=pl.BlockSpec((None, 8, tn), lambda l, j: (l, 0, j)),
        out_shape=jax.ShapeDtypeStruct((depth, 8, d6), F32),
        compiler_params=_cparams(2),
        name="ada_mod",
    )(cv, ada_w, ada_b.reshape(depth, 1, d6))


def _modnorm(x, gain, shift, scale):
    ms = jnp.mean(x * x, axis=-1, keepdims=True)
    return x * lax.rsqrt(ms + EPS) * gain * (1.0 + scale) + shift


def _in_proj_kernel(*refs, rope):
    if rope:
        (x_ref, mod_ref, g_ref, wf_ref, wk_ref, wqt_ref, wvt_ref, wcp_ref,
         kc_ref, ksa_ref, ksb_ref, qc_ref, qs_ref,
         uf_ref, k_ref, qt_ref, vt_ref, ucp_ref) = refs
    else:
        (x_ref, mod_ref, g_ref, wf_ref, wk_ref, wqt_ref, wvt_ref, wcp_ref,
         uf_ref, k_ref, qt_ref, vt_ref, ucp_ref) = refs
    h = _modnorm(x_ref[...], g_ref[...], mod_ref[0:1, :], mod_ref[1:2, :]).astype(BF16)
    tm = h.shape[0]

    uf_ref[...] = _dot(h, wf_ref[...])
    ucp_ref[...] = _dot(h, wcp_ref[...]).astype(BF16)
    vt = _dot_nt(wvt_ref[...], h).astype(BF16)
    for hd in range(N_HEADS):
        vt_ref[hd * V_ROWS:hd * V_ROWS + HEAD_W, :] = vt[hd * HEAD_W:(hd + 1) * HEAD_W]
        vt_ref[hd * V_ROWS + HEAD_W:(hd + 1) * V_ROWS, :] = jnp.ones((ONES_ROWS, tm), BF16)

    k = _dot(h, wk_ref[...])
    qt = _dot_nt(wqt_ref[...], h)
    if rope:
        kc, ksa, ksb = kc_ref[...], ksa_ref[...], ksb_ref[...]
        for hd in range(N_HEADS):
            kh = k[:, hd * HEAD_W:(hd + 1) * HEAD_W]
            kh = kh * kc + pltpu.roll(kh, HEAD_W - ROPE_HALF, 1) * ksb + pltpu.roll(kh, ROPE_HALF, 1) * ksa
            k_ref[:, hd * HEAD_W:(hd + 1) * HEAD_W] = kh.astype(BF16)
        q3 = qt.reshape(2 * N_HEADS, QK_DIM, tm)
        q3s = jnp.concatenate([q3[:, ROPE_HALF:, :], q3[:, :ROPE_HALF, :]], axis=1)
        q3 = q3 * qc_ref[...][None] + q3s * qs_ref[...][None]
        qt = q3.reshape(2 * N_HEADS * QK_DIM, tm)
    else:
        k_ref[...] = k.astype(BF16)
    qt_ref[...] = (qt * (ATT_SCALE * LOG2E)).astype(BF16)


def _in_proj(x, mod, gain, w, rope_tabs, *, tm, seq_len):
    n, d = x.shape
    tps = seq_len // tm
    nb = mod.shape[0]
    rope = rope_tabs is not None
    mod_map = (lambda i: (i // tps, 0, 0)) if nb > 1 else (lambda i: (0, 0, 0))
    in_specs = [pl.BlockSpec((tm, d), lambda i: (i, 0)),
                pl.BlockSpec((None, 6, d), mod_map),
                _resident((1, d))] + [_resident(a.shape) for a in w]
    args = [x, mod, gain.reshape(1, d)] + list(w)
    if rope:
        kc, ksa, ksb, qc, qs = rope_tabs
        in_specs += [pl.BlockSpec((tm, HEAD_W), lambda i: (i % tps, 0))] * 3
        in_specs += [pl.BlockSpec((QK_DIM, tm), lambda i: (0, i % tps))] * 2
        args += [kc, ksa, ksb, qc, qs]
    wf, wk, wqt, wvt, wcp = w
    out_shape = (jax.ShapeDtypeStruct((n, wf.shape[1]), F32),
                 jax.ShapeDtypeStruct((n, wk.shape[1]), BF16),
                 jax.ShapeDtypeStruct((wqt.shape[0], n), BF16),
                 jax.ShapeDtypeStruct((N_HEADS * V_ROWS, n), BF16),
                 jax.ShapeDtypeStruct((n, wcp.shape[1]), BF16))
    out_specs = (pl.BlockSpec((tm, wf.shape[1]), lambda i: (i, 0)),
                 pl.BlockSpec((tm, wk.shape[1]), lambda i: (i, 0)),
                 pl.BlockSpec((wqt.shape[0], tm), lambda i: (0, i)),
                 pl.BlockSpec((N_HEADS * V_ROWS, tm), lambda i: (0, i)),
                 pl.BlockSpec((tm, wcp.shape[1]), lambda i: (i, 0)))
    return pl.pallas_call(
        functools.partial(_in_proj_kernel, rope=rope),
        grid=(n // tm,), in_specs=in_specs, out_specs=out_specs, out_shape=out_shape,
        compiler_params=_cparams(1), name="in_proj_rope" if rope else "in_proj",
    )(*args)


def _fnet_kernel(x_ref, t1_ref, g32_ref, cs32_ref, o_ref, xs_ref, zz_ref, ys_ref, g_ref, cs_ref, *,
                 n2, scale):
    ch = x_ref.shape[1]
    xp, zp, yp = DFT_N1 + PITCH_PAD, 2 * DFT_N1 + PITCH_PAD, n2 + PITCH_PAD
    g_ref[...] = g32_ref[...].astype(BF16)
    cs_ref[...] = cs32_ref[...].astype(BF16)

    def copy_in(b, carry):
        src = pl.multiple_of(b * DFT_N1, DFT_N1)
        dst = pl.multiple_of(b * xp, PITCH_PAD)
        xs_ref[pl.ds(dst, DFT_N1), :] = x_ref[pl.ds(src, DFT_N1), :]
        return carry

    lax.fori_loop(0, n2, copy_in, 0, unroll=8)

    def stage1(n1, carry):
        xs = xs_ref[pl.ds(n1, n2, stride=xp), :].astype(BF16)
        z = _dot(t1_ref[n1].astype(BF16), xs)
        zz_ref[pl.ds(n1, n2, stride=zp), :] = z[:n2]
        zz_ref[pl.ds(DFT_N1 + n1, n2, stride=zp), :] = z[n2:]
        return carry

    lax.fori_loop(0, DFT_N1, stage1, 0, unroll=8)

    def stage2(kp, carry):
        base = pl.multiple_of(2 * kp * zp, PITCH_PAD)
        zk = jnp.concatenate([zz_ref[pl.ds(base, 2 * DFT_N1), :],
                              zz_ref[pl.ds(base + zp, 2 * DFT_N1), :]], axis=1).astype(BF16)
        p = _dot(g_ref[...], zk)
        pc = jnp.concatenate(
            [jnp.concatenate([p[:DFT_N1, h * ch:(h + 1) * ch], p[DFT_N1:, h * ch:(h + 1) * ch]], axis=1)
             for h in range(2)], axis=0).astype(BF16)
        y = _dot(pc, cs_ref[...]) * scale
        ys_ref[pl.ds(2 * kp, DFT_N1, stride=yp), :] = y[:DFT_N1]
        ys_ref[pl.ds(2 * kp + 1, DFT_N1, stride=yp), :] = y[DFT_N1:]
        return carry

    lax.fori_loop(0, n2 // 2, stage2, 0, unroll=4)

    def copy_out(k1, carry):
        src = pl.multiple_of(k1 * yp, PITCH_PAD)
        dst = pl.multiple_of(k1 * n2, PITCH_PAD)
        o_ref[pl.ds(dst, n2), :] = ys_ref[pl.ds(src, n2), :]
        return carry

    lax.fori_loop(0, DFT_N1, copy_out, 0, unroll=8)


def _channel_dft_mats(ch):
    c = np.arange(F_GROUP)
    ang = 2.0 * np.pi * np.outer(c, c) / F_GROUP
    eye = np.eye(ch // F_GROUP)
    return np.kron(eye, np.cos(ang)), np.kron(eye, np.sin(ang))


def _fnet(u, *, seq_len):
    n, width = u.shape
    nb = n // seq_len
    ch = 128
    n2 = seq_len // DFT_N1
    n1i, k2i, n2i = np.arange(DFT_N1), np.arange(n2), np.arange(n2)
    theta = 2.0 * np.pi * (n1i[:, None, None] * k2i[None, :, None] / seq_len
                           + k2i[None, :, None] * n2i[None, None, :] / n2)
    t1 = np.concatenate([np.cos(theta), -np.sin(theta)], axis=1)
    k1i = np.arange(DFT_N1)
    phi = 2.0 * np.pi * np.outer(k1i, n1i) / DFT_N1
    cg, sg = np.cos(phi), np.sin(phi)
    gmat = np.block([[cg, sg], [-sg, cg]])
    cbd, sbd = _channel_dft_mats(ch)
    cs = np.concatenate([cbd, sbd], axis=0)
    scale = 1.0 / math.sqrt(seq_len * F_GROUP)
    return pl.pallas_call(
        functools.partial(_fnet_kernel, n2=n2, scale=scale),
        grid=(nb, width // ch),
        in_specs=[pl.BlockSpec((seq_len, ch), lambda b, c: (b, c)),
                  _resident(t1.shape), _resident(gmat.shape), _resident(cs.shape)],
        out_specs=pl.BlockSpec((seq_len, ch), lambda b, c: (b, c)),
        out_shape=jax.ShapeDtypeStruct((n, width), F32),
        scratch_shapes=[pltpu.VMEM((n2 * (DFT_N1 + PITCH_PAD), ch), F32),
                        pltpu.VMEM((n2 * (2 * DFT_N1 + PITCH_PAD), ch), F32),
                        pltpu.VMEM((DFT_N1 * (n2 + PITCH_PAD), ch), F32),
                        pltpu.VMEM(gmat.shape, BF16), pltpu.VMEM(cs.shape, BF16)],
        compiler_params=_cparams(2), name="fnet",
    )(u, jnp.asarray(t1, F32), jnp.asarray(gmat, F32), jnp.asarray(cs, F32))


def _fnet_dense_kernel(x_ref, f_ref, cs_ref, o_ref, *, scale):
    p = _dot(f_ref[...].astype(BF16), x_ref[...].astype(BF16))
    l = p.shape[0] // 2
    pc = jnp.concatenate([p[:l], p[l:]], axis=1).astype(BF16)
    o_ref[...] = _dot(pc, cs_ref[...].astype(BF16)) * scale


def _fnet_dense(u, *, seq_len):
    n, width = u.shape
    pos = np.arange(seq_len)
    ang = 2.0 * np.pi * np.outer(pos, pos) / seq_len
    fmat = np.concatenate([np.cos(ang), np.sin(ang)], axis=0)
    cbd, sbd = _channel_dft_mats(width)
    cs = np.concatenate([cbd, -sbd], axis=0)
    scale = 1.0 / math.sqrt(seq_len * F_GROUP)
    return pl.pallas_call(
        functools.partial(_fnet_dense_kernel, scale=scale),
        grid=(n // seq_len,),
        in_specs=[pl.BlockSpec((seq_len, width), lambda b: (b, 0)),
                  _resident(fmat.shape), _resident(cs.shape)],
        out_specs=pl.BlockSpec((seq_len, width), lambda b: (b, 0)),
        out_shape=jax.ShapeDtypeStruct((n, width), F32),
        compiler_params=_cparams(1), name="fnet_dense",
    )(u, jnp.asarray(fmat, F32), jnp.asarray(cs, F32))


def _masked_queries(qt_ref):
    qt = qt_ref[...]
    row = lax.broadcasted_iota(jnp.int32, qt.shape, 0)
    zero = jnp.zeros_like(qt)
    return jnp.concatenate([jnp.where(row < QK_DIM, qt, zero), jnp.where(row >= QK_DIM, qt, zero)], axis=1)


def _key_chunks(kv_refs, tk):
    chunks = []
    for k_ref, vt_ref in kv_refs:
        n = k_ref.shape[0]
        step = min(tk, n)
        chunks += [(k_ref, vt_ref, slice(c * step, (c + 1) * step)) for c in range(n // step)]
    return chunks


PV_LAG = 2


def _attn_tiles(qm, kv_refs, *, tk):
    n_t = len(qm)
    m = [jnp.full((1, qm[0].shape[1]), -jnp.inf, F32) for _ in range(n_t)]
    acc = [jnp.zeros((V_ROWS, qm[0].shape[1]), F32) for _ in range(n_t)]
    pending = []

    def value_matmul(t, vt_ref, rows, e, alpha):
        acc[t] = _dot(vt_ref[:, rows], e) + acc[t] * alpha

    for k_ref, vt_ref, rows in _key_chunks(kv_refs, tk):
        for t in range(n_t):
            s = _dot(k_ref[rows, :], qm[t]).astype(BF16)
            if len(pending) >= PV_LAG:
                value_matmul(*pending.pop(0))
            m_new = jnp.maximum(m[t], jnp.max(s, axis=0, keepdims=True).astype(F32))
            alpha = jnp.exp2(m[t] - m_new)
            e = jnp.exp2(s - m_new.astype(BF16))
            m[t] = m_new
            pending.append((t, vt_ref, rows, e, alpha))
    for p in pending:
        value_matmul(*p)
    return acc


def _attn_finish(acc, lam, sg_ref, o_ref, *, lam_init):
    tq = acc.shape[1] // 2
    inv = 1.0 / acc[HEAD_W:HEAD_W + 1, :]
    acc = acc[:HEAD_W]
    o = acc[:, :tq] * inv[:, :tq] - lam * (acc[:, tq:] * inv[:, tq:])
    ms = jnp.mean(o * o, axis=0, keepdims=True)
    y = o * lax.rsqrt(ms + EPS) * sg_ref[...] * (1.0 - lam_init)
    o_ref[...] = y.T.astype(BF16)


def _attn_kernel(*refs, n_src, n_tiles, lam_init, tk):
    q_ref = refs[0]
    kv_refs = [(refs[1 + 2 * s], refs[2 + 2 * s]) for s in range(n_src)]
    lq1_ref, lk1_ref, lq2_ref, lk2_ref, sg_ref, o_ref = refs[1 + 2 * n_src:]
    tq = q_ref.shape[1] // n_tiles
    lam = (jnp.exp(jnp.sum(lq1_ref[...] * lk1_ref[...], axis=1, keepdims=True))
           - jnp.exp(jnp.sum(lq2_ref[...] * lk2_ref[...], axis=1, keepdims=True)) + lam_init)
    qm = [_masked_queries(q_ref.at[:, t * tq:(t + 1) * tq]) for t in range(n_tiles)]
    acc = _attn_tiles(qm, kv_refs, tk=tk)
    for t in range(n_tiles):
        _attn_finish(acc[t], lam, sg_ref, o_ref.at[t * tq:(t + 1) * tq, :], lam_init=lam_init)


def _attention(qt, kv, lam_vecs, subln_g, *, lq, tq, n_tiles, tk, lam_init):
    n = qt.shape[1]
    nb = n // lq
    steps = lq // (n_tiles * tq)
    assert lq % (n_tiles * tq) == 0
    vec = pl.BlockSpec((1, QK_DIM), lambda b, h, j: (0, 0))
    kv_specs, kv_args = [], []
    for k, vt in kv:
        ls = k.shape[0] // nb
        assert ls % min(tk, ls) == 0
        kv_specs += [pl.BlockSpec((ls, HEAD_W), lambda b, h, j: (b, h)),
                     pl.BlockSpec((V_ROWS, ls), lambda b, h, j: (h, b))]
        kv_args += [k, vt]
    return pl.pallas_call(
        functools.partial(_attn_kernel, n_src=len(kv), n_tiles=n_tiles, lam_init=lam_init, tk=tk),
        grid=(nb, N_HEADS, steps),
        in_specs=[pl.BlockSpec((HEAD_W, n_tiles * tq), lambda b, h, j: (h, b * steps + j))] + kv_specs
                 + [vec, vec, vec, vec, pl.BlockSpec((HEAD_W, 1), lambda b, h, j: (0, 0))],
        out_specs=pl.BlockSpec((n_tiles * tq, HEAD_W), lambda b, h, j: (b * steps + j, h)),
        out_shape=jax.ShapeDtypeStruct((n, N_HEADS * HEAD_W), BF16),
        compiler_params=_cparams(3), name="diff_attn",
    )(qt, *kv_args, *[v.reshape(1, QK_DIM) for v in lam_vecs], subln_g.reshape(HEAD_W, 1))


def _seq_flags(i, tps):
    t = i % tps
    return t == 0, t == tps - 1


def _row_windows(ref, tm):
    sub = 8
    copies = {}

    def window(start):
        r, a = start % sub, start // sub
        assert sub * a + tm <= tm + 2 * HALO - sub
        if r not in copies:
            copies[r] = ref[pl.ds(r, tm + 2 * HALO - sub), :]
        return copies[r][sub * a:sub * a + tm]

    return window


def _local_kernel(u_ref, up_ref, un_ref, dww_ref, dwb_ref, lng_ref, lnb_ref, pw_ref, ps_ref,
                  cc_ref, pp_ref, zc_ref, zp_ref, *, tps, seq_len):
    tm = u_ref.shape[0]
    cw = cc_ref.shape[1]
    first, last = _seq_flags(pl.program_id(0), tps)

    def split(u):
        u = u.astype(F32)
        return u[:, :cw] * jax.nn.sigmoid(u[:, cw:2 * cw]), u[:, 2 * cw:]

    zc, zp = split(u_ref[...])
    zc_ref[HALO:HALO + tm, :] = zc
    zp_ref[HALO:HALO + tm, :] = zp
    zc, zp = split(up_ref[...])
    keep = jnp.where(first, 0.0, 1.0)
    zc_ref[0:HALO, :] = zc * keep
    zp_ref[0:HALO, :] = zp * keep
    zc, zp = split(un_ref[...])
    keep = jnp.where(last, 0.0, 1.0)
    zc_ref[HALO + tm:, :] = zc * keep
    zp_ref[HALO + tm:, :] = zp * keep

    pad = (CONV_TAPS - 1) // 2
    conv_rows = _row_windows(zc_ref, tm)
    acc = jnp.zeros((tm, cw), F32) + dwb_ref[...]
    for r in range(8):
        part = None
        for j in range(CONV_TAPS):
            if (HALO - pad + j) % 8 == r:
                term = conv_rows(HALO - pad + j) * dww_ref[j:j + 1, :]
                part = term if part is None else part + term
        acc = acc + part
    mu = jnp.mean(acc, axis=-1, keepdims=True)
    var = jnp.mean(jnp.square(acc - mu), axis=-1, keepdims=True)
    y = (acc - mu) * lax.rsqrt(var + EPS) * lng_ref[...] + lnb_ref[...]
    cc_ref[...] = (y * jax.nn.sigmoid(y)).astype(BF16)

    pool_rows = _row_windows(zp_ref, tm)

    def shifted(d):
        return pool_rows(HALO + d)

    centre = shifted(0)
    sums = []
    s = None
    prev_half = 0
    for win in POOL_WINDOWS:
        half = win // 2
        for d in list(range(-half, -prev_half)) + list(range(prev_half, half)):
            term = centre if d == 0 else shifted(d)
            s = term if s is None else s + term
        sums.append(s)
        prev_half = half
    pos = (pl.program_id(0) % tps) * tm + lax.broadcasted_iota(jnp.int32, (tm, 1), 0)
    lane_group = lax.broadcasted_iota(jnp.int32, (tm, cw), 1) // POOL_GROUP
    mean = None
    for g, win in enumerate(POOL_WINDOWS):
        half = win // 2
        hi = jnp.minimum(pos + (win - half - 1), seq_len - 1)
        lo = jnp.maximum(pos - half, 0)
        mg = sums[g] / (hi - lo + 1).astype(F32)
        mean = mg if mean is None else jnp.where(lane_group == g, mg, mean)
    d = (mean - centre).astype(BF16)
    pp_ref[...] = (_dot(d, pw_ref[...]) * ps_ref[...]).astype(BF16)


def _local_mix(ucp, dw_w, dw_b, ln_g, ln_b, pool_bd, pool_scale, *, tm, seq_len):
    n, w3 = ucp.shape
    cw = w3 // 3
    tps = seq_len // tm
    r = tm // HALO
    nblk = n // HALO
    row = lambda a: a.reshape(1, cw)
    return pl.pallas_call(
        functools.partial(_local_kernel, tps=tps, seq_len=seq_len),
        grid=(n // tm,),
        in_specs=[pl.BlockSpec((tm, w3), lambda i: (i, 0)),
                  pl.BlockSpec((HALO, w3), lambda i: (jnp.maximum(i * r - 1, 0), 0)),
                  pl.BlockSpec((HALO, w3), lambda i: (jnp.minimum((i + 1) * r, nblk - 1), 0)),
                  _resident((CONV_TAPS, cw)), _resident((1, cw)), _resident((1, cw)), _resident((1, cw)),
                  _resident((cw, cw)), _resident((1, cw))],
        out_specs=(pl.BlockSpec((tm, cw), lambda i: (i, 0)), pl.BlockSpec((tm, cw), lambda i: (i, 0))),
        out_shape=(jax.ShapeDtypeStruct((n, cw), BF16), jax.ShapeDtypeStruct((n, cw), BF16)),
        scratch_shapes=[pltpu.VMEM((tm + 2 * HALO, cw), F32), pltpu.VMEM((tm + 2 * HALO, cw), F32)],
        compiler_params=_cparams(1), name="local_mix",
    )(ucp, ucp, ucp, dw_w, row(dw_b), row(ln_g), row(ln_b), pool_bd, row(pool_scale))


def _merge_kernel(x_ref, mod_ref, g_ref, fm_ref, oa_ref, cc_ref, pp_ref,
                  wg_ref, wof_ref, woa_ref, woc_ref, wop_ref, wout_ref, o_ref):
    d = x_ref.shape[1]
    x = x_ref[...]
    h = _modnorm(x, g_ref[...], mod_ref[0:1, :], mod_ref[1:2, :]).astype(BF16)
    branches = ((fm_ref[...].astype(BF16), wof_ref), (oa_ref[...], woa_ref),
                (cc_ref[...], woc_ref), (pp_ref[...], wop_ref))
    y = None
    for i, (act, w_ref) in enumerate(branches):
        gate = jax.nn.sigmoid(_dot(h, wg_ref[:, i * d:(i + 1) * d]))
        t = gate * _dot(act, w_ref[...])
        y = t if y is None else y + t
    z = _dot(y.astype(BF16), wout_ref[...])
    o_ref[...] = x + mod_ref[2:3, :] * z


def _merge(x, mod, gain, fm, oa, cc, pp, w, *, tm, seq_len):
    n, d = x.shape
    tps = seq_len // tm
    nb = mod.shape[0]
    mod_map = (lambda i: (i // tps, 0, 0)) if nb > 1 else (lambda i: (0, 0, 0))
    tile = lambda a: pl.BlockSpec((tm, a.shape[1]), lambda i: (i, 0))
    return pl.pallas_call(
        _merge_kernel,
        grid=(n // tm,),
        in_specs=[tile(x), pl.BlockSpec((None, 6, d), mod_map), _resident((1, d)),
                  tile(fm), tile(oa), tile(cc), tile(pp)] + [_resident(a.shape) for a in w],
        out_specs=tile(x),
        out_shape=jax.ShapeDtypeStruct((n, d), F32),
        compiler_params=_cparams(1), name="merge",
    )(x, mod, gain.reshape(1, d), fm, oa, cc, pp, *w)


def _ffn_kernel(x_ref, xp_ref, xn_ref, mod_ref, g_ref, wv_ref, wg_ref, dww_ref, dwb_ref, wd_ref, fg_ref,
                o_ref, h_ref, gs_ref, mid_ref, *, tps, final):
    tm = x_ref.shape[0]
    first, last = _seq_flags(pl.program_id(0), tps)
    gain, shift, scale = g_ref[...], mod_ref[3:4, :], mod_ref[4:5, :]
    x = x_ref[...]
    h_ref[HALO:HALO + tm, :] = _modnorm(x, gain, shift, scale).astype(BF16)
    h_ref[0:HALO, :] = _modnorm(xp_ref[...], gain, shift, scale).astype(BF16)
    h_ref[HALO + tm:, :] = _modnorm(xn_ref[...], gain, shift, scale).astype(BF16)
    keep_p = jnp.where(first, 0.0, 1.0)
    keep_n = jnp.where(last, 0.0, 1.0)

    dff = wv_ref.shape[1]
    for c in range(dff // FFN_CHUNK):
        cols = slice(c * FFN_CHUNK, (c + 1) * FFN_CHUNK)
        val = _dot(h_ref[HALO:HALO + tm, :], wv_ref[:, cols])
        gt = _dot(h_ref[...], wg_ref[:, cols])
        gs_ref[HALO:HALO + tm, :] = gt[HALO:HALO + tm]
        gs_ref[0:HALO, :] = gt[:HALO] * keep_p
        gs_ref[HALO + tm:, :] = gt[HALO + tm:] * keep_n
        gc = dwb_ref[:, cols] + gs_ref[pl.ds(HALO, tm), :] * dww_ref[1:2, cols]
        gc = gc + gs_ref[pl.ds(HALO - 1, tm), :] * dww_ref[0:1, cols]
        gc = gc + gs_ref[pl.ds(HALO + 1, tm), :] * dww_ref[2:3, cols]
        act = 0.5 * gc * (1.0 + lax.erf(gc * (1.0 / math.sqrt(2.0))))
        mid_ref[:, cols] = (val * act).astype(BF16)

    y = x + mod_ref[5:6, :] * _dot(mid_ref[...], wd_ref[...])
    if final:
        ms = jnp.mean(y * y, axis=-1, keepdims=True)
        y = y * lax.rsqrt(ms + EPS) * fg_ref[...]
    o_ref[...] = y


def _ffn(x, mod, gain, wv, wg, dw_w, dw_b, wd, final_g, *, tm, seq_len, final):
    n, d = x.shape
    dff = wv.shape[1]
    tps = seq_len // tm
    nb = mod.shape[0]
    r = tm // HALO
    nblk = n // HALO
    mod_map = (lambda i: (i // tps, 0, 0)) if nb > 1 else (lambda i: (0, 0, 0))
    return pl.pallas_call(
        functools.partial(_ffn_kernel, tps=tps, final=final),
        grid=(n // tm,),
        in_specs=[pl.BlockSpec((tm, d), lambda i: (i, 0)),
                  pl.BlockSpec((HALO, d), lambda i: (jnp.maximum(i * r - 1, 0), 0)),
                  pl.BlockSpec((HALO, d), lambda i: (jnp.minimum((i + 1) * r, nblk - 1), 0)),
                  pl.BlockSpec((None, 6, d), mod_map),
                  _resident((1, d)), _resident(wv.shape), _resident(wg.shape),
                  _resident((FFN_TAPS, dff)), _resident((1, dff)), _resident(wd.shape), _resident((1, d))],
        out_specs=pl.BlockSpec((tm, d), lambda i: (i, 0)),
        out_shape=jax.ShapeDtypeStruct((n, d), F32),
        scratch_shapes=[pltpu.VMEM((tm + 2 * HALO, d), BF16),
                        pltpu.VMEM((tm + 2 * HALO, FFN_CHUNK), F32),
                        pltpu.VMEM((tm, dff), BF16)],
        compiler_params=_cparams(1), name="ffn_final" if final else "ffn",
    )(x, x, x, mod, gain.reshape(1, d), wv, wg, dw_w, dw_b.reshape(1, dff), wd, final_g.reshape(1, d))


def _rope_perm():
    q = QK_DIM // 4
    one = np.concatenate([np.arange(0, q), np.arange(2 * q, 3 * q), np.arange(q, 2 * q), np.arange(3 * q, 4 * q)])
    return np.concatenate([g * QK_DIM + one for g in range(2 * N_HEADS)])


def _rope_tables(seq_len):
    nf = QK_DIM // 4
    inv = ROPE_BASE ** (-np.arange(nf, dtype=np.float64) / nf)
    t = np.arange(seq_len)
    ang = np.concatenate([(t // GRID_W)[:, None] * inv, (t % GRID_W)[:, None] * inv], axis=1)
    cos, sin, zero = np.cos(ang), np.sin(ang), np.zeros_like(ang)
    rep = HEAD_W // QK_DIM
    kc = np.tile(np.concatenate([cos, cos], axis=1), (1, rep))
    ksa = np.tile(np.concatenate([zero, sin], axis=1), (1, rep))
    ksb = np.tile(np.concatenate([-sin, zero], axis=1), (1, rep))
    qc = np.concatenate([cos, cos], axis=1).T
    qs = np.concatenate([-sin, sin], axis=1).T
    return tuple(jnp.asarray(a, F32) for a in (kc, ksa, ksb, qc, qs))


def kernel(x, c, ctx, c_ctx, norm1_g, norm2_g, ada_w, ada_b, w_in, lam_q1, lam_k1, lam_q2, lam_k2, subln_g, conv_dw_w, conv_dw_b, conv_ln_g, conv_ln_b, pool_w, pool_scale, wo_f, wo_a, wo_c, wo_p, w_out, w_up, ffn_dw_w, ffn_dw_b, w_down, final_g):
    bsz, seq, d = x.shape
    n_ctx = ctx.shape[1]
    depth = w_in.shape[0]
    d_ff = w_down.shape[1]
    f_w = wo_f.shape[1]
    a_w = wo_a.shape[1]
    c_w = wo_c.shape[1]
    p_w = wo_p.shape[1]
    off_q = f_w
    off_k = off_q + a_w
    off_v = off_k + a_w
    off_c = off_v + a_w
    off_g = off_c + 2 * c_w + p_w
    assert seq % (8 * DFT_N1) == 0 and n_ctx % 256 == 0
    assert a_w == N_HEADS * HEAD_W and d_ff % FFN_CHUNK == 0
    tm_x, tm_c = 512, 256

    cv = jnp.concatenate([c, c_ctx[None, :], jnp.zeros((8 - bsz - 1, d), F32)], axis=0)
    mods = _ada_mod(cv, ada_w, ada_b)
    rope_tabs = _rope_tables(seq)
    perm = _rope_perm()

    xs = x.reshape(bsz * seq, d)
    cs = ctx.reshape(bsz * n_ctx, d)
    out = None
    for l in range(depth):
        last = l == depth - 1
        lam_init = 0.8 - 0.6 * math.exp(-0.3 * l)
        mod_x = mods[l, :bsz].reshape(bsz, 6, d)
        mod_c = mods[l, bsz:bsz + 1].reshape(1, 6, d)
        wl = w_in[l]
        w_a = (wl[:, :off_q].astype(BF16),
               wl[:, off_k:off_v][:, perm].astype(BF16),
               wl[:, off_q:off_k][:, perm].T.astype(BF16),
               wl[:, off_v:off_c].T.astype(BF16),
               wl[:, off_c:off_g].astype(BF16))
        lam_vecs = (lam_q1[l], lam_k1[l], lam_q2[l], lam_k2[l])
        pool_bd = jax.scipy.linalg.block_diag(*[pool_w[l, g] for g in range(len(POOL_WINDOWS))]).astype(BF16)
        w_m = (wl[:, off_g:].astype(BF16), wo_f[l].astype(BF16), wo_a[l].astype(BF16), wo_c[l].astype(BF16),
               wo_p[l].astype(BF16), w_out[l].astype(BF16))
        wv_up = w_up[l][:, :d_ff].astype(BF16)
        wg_up = w_up[l][:, d_ff:].astype(BF16)
        wd = w_down[l].astype(BF16)
        local_w = (conv_dw_w[l], conv_dw_b[l], conv_ln_g[l], conv_ln_b[l], pool_bd, pool_scale[l])

        uf_c, k_c, qt_c, vt_c, ucp_c = _in_proj(cs, mod_c, norm1_g[l], w_a, None, tm=tm_c, seq_len=n_ctx)
        uf_x, k_x, qt_x, vt_x, ucp_x = _in_proj(xs, mod_x, norm1_g[l], w_a, rope_tabs, tm=tm_x, seq_len=seq)
        oa_x = _attention(qt_x, [(k_c, vt_c), (k_x, vt_x)], lam_vecs, subln_g[l],
                          lq=seq, tq=256, n_tiles=4, tk=ATTN_KEY_CHUNK, lam_init=lam_init)
        fm_x = _fnet(uf_x, seq_len=seq)
        cc_x, pp_x = _local_mix(ucp_x, *local_w, tm=tm_x, seq_len=seq)
        xs = _merge(xs, mod_x, norm1_g[l], fm_x, oa_x, cc_x, pp_x, w_m, tm=tm_x, seq_len=seq)
        xs = _ffn(xs, mod_x, norm2_g[l], wv_up, wg_up, ffn_dw_w[l], ffn_dw_b[l], wd, final_g,
                  tm=tm_x, seq_len=seq, final=last)

        if not last:
            oa_c = _attention(qt_c, [(k_c, vt_c)], lam_vecs, subln_g[l],
                              lq=n_ctx, tq=128, n_tiles=2, tk=ATTN_KEY_CHUNK, lam_init=lam_init)
            fm_c = _fnet_dense(uf_c, seq_len=n_ctx)
            cc_c, pp_c = _local_mix(ucp_c, *local_w, tm=tm_c, seq_len=n_ctx)
            cs = _merge(cs, mod_c, norm1_g[l], fm_c, oa_c, cc_c, pp_c, w_m, tm=tm_c, seq_len=n_ctx)
            cs = _ffn(cs, mod_c, norm2_g[l], wv_up, wg_up, ffn_dw_w[l], ffn_dw_b[l], wd, final_g,
                      tm=tm_c, seq_len=n_ctx, final=False)
    return xs.reshape(bsz, seq, d)
```
